```python
import jax, jax.numpy as jnp
from jax import lax
import numpy as np

D_MODEL = 2048
BATCH = 8
SEQ = 2048
DEPTH = 2
DEC_BATCH = 32
DEC_SEQ = 8
PAST_LEN = 8192
PAGE_SIZE = 128

HEAD_DIM = 64
W_SB = D_MODEL // 2
W_RET = D_MODEL // 4
W_POOL = D_MODEL - W_SB - W_RET
H_SB = W_SB // HEAD_DIM
H_RET = W_RET // HEAD_DIM
POOL_WINDOWS = (2, 4, 8, 16)
N_POOL_GROUPS = len(POOL_WINDOWS)
POOL_GROUP_W = W_POOL // N_POOL_GROUPS
POOL_BUF = max(POOL_WINDOWS) - 1
MIX_W = W_SB + W_RET + W_POOL
IN_W = 3 * W_SB + 4 * W_RET + W_POOL
Q_BLOCK = 128
RET_CHUNK = 128
ROPE_THETA = 10000.0
D_FF = 5632
N_EXPERTS = 8
TOP_K = 2
D_FF_EXPERT = 7168
N_DENSE = (DEPTH + 1) // 2
N_MOE = DEPTH // 2
SB_BIAS_HI = -2.0
SB_BIAS_LO = -9.0
EPS = 1e-6

kernel_name = "hymba_stickbreak_retention_pool_moe_step"


def rms_norm(x, g):
    xf = x.astype(jnp.float32)
    y = xf * lax.rsqrt(jnp.mean(xf * xf, axis=-1, keepdims=True) + EPS)
    return (y * g.astype(jnp.float32)).astype(x.dtype)


def rope(x, pos):
    half = HEAD_DIM // 2
    inv = ROPE_THETA ** (-jnp.arange(half, dtype=jnp.float32) / half)
    ang = pos.astype(jnp.float32)[:, None] * inv[None, :]
    cos = jnp.cos(ang)[None, :, None, :]
    sin = jnp.sin(ang)[None, :, None, :]
    xf = x.astype(jnp.float32)
    x1, x2 = xf[..., :half], xf[..., half:]
    return jnp.concatenate([x1 * cos - x2 * sin, x1 * sin + x2 * cos], axis=-1).astype(x.dtype)


def stick_breaking(q, k, v, q_start, bias):
    T = q.shape[1]
    scale = HEAD_DIM ** -0.5
    b_h = bias.astype(jnp.float32)[None, :, None, None]
    outs = []
    for b0 in range(0, T, Q_BLOCK):
        b1 = min(b0 + Q_BLOCK, T)
        k_end = max(q_start + b1 - 1, 1)
        qb, kb, vb = q[:, b0:b1], k[:, :k_end], v[:, :k_end]
        q_pos = q_start + jnp.arange(b0, b1)
        k_pos = jnp.arange(k_end)
        z = jnp.einsum('bqhd,bkhd->bhqk', qb, kb, preferred_element_type=jnp.float32) * scale + b_h
        mask = k_pos[None, :] < q_pos[:, None]
        ls_skip = jnp.where(mask, jax.nn.log_sigmoid(-z), 0.0)
        tail = lax.cumsum(ls_skip, axis=3, reverse=True) - ls_skip
        a = jnp.where(mask, jnp.exp(jax.nn.log_sigmoid(z) + tail), 0.0)
        outs.append(jnp.einsum('bhqk,bkhd->bqhd', a.astype(v.dtype), vb,
                               preferred_element_type=jnp.float32))
    return jnp.concatenate(outs, axis=1)


def retention(q, k, v, s0):
    B, T = q.shape[0], q.shape[1]
    C = RET_CHUNK if T % RET_CHUNK == 0 else T
    nc = T // C
    log_g = jnp.log(1.0 - 2.0 ** (-5.0 - jnp.arange(H_RET, dtype=jnp.float32)))
    idx = jnp.arange(C, dtype=jnp.float32)
    diff = idx[:, None] - idx[None, :]
    decay = jnp.where(diff >= 0, jnp.exp(log_g[:, None, None] * jnp.maximum(diff, 0.0)), 0.0)
    q_dec = jnp.exp(log_g[:, None] * (idx + 1.0))[None, :, :, None]
    k_dec = jnp.exp(log_g[:, None] * (C - 1.0 - idx))[None, :, :, None]
    chunk_dec = jnp.exp(log_g * C)[None, :, None, None]

    def to_chunks(x):
        return x.astype(jnp.float32).reshape(B, nc, C, H_RET, HEAD_DIM).transpose(1, 0, 3, 2, 4)

    def step(s, inp):
        qc, kc, vc = inp
        att = jnp.einsum('bhid,bhjd->bhij', qc, kc) * decay[None]
        o = jnp.einsum('bhij,bhje->bhie', att, vc) + jnp.einsum('bhid,bhde->bhie', qc, s) * q_dec
        s = s * chunk_dec + jnp.einsum('bhjd,bhje->bhde', kc * k_dec, vc)
        return s, o

    s_fin, o = lax.scan(step, s0.astype(jnp.float32), (to_chunks(q), to_chunks(k), to_chunks(v)))
    o = o.transpose(1, 0, 3, 2, 4).reshape(B, T, H_RET, HEAD_DIM)
    return o, s_fin


def pool_mixer(u, buf, start, w_pool_l, scale_l):
    B, T = u.shape[0], u.shape[1]
    ext = jnp.concatenate([buf.astype(jnp.float32), u.astype(jnp.float32)], axis=1)
    pos = start - POOL_BUF + jnp.arange(POOL_BUF + T)
    valid = (pos >= 0).astype(jnp.float32)
    cs = jnp.concatenate([jnp.zeros((B, 1, W_POOL), jnp.float32),
                          jnp.cumsum(ext * valid[None, :, None], axis=1)], axis=1)
    cnt = jnp.concatenate([jnp.zeros((1,), jnp.float32), jnp.cumsum(valid)])
    hi = POOL_BUF + 1
    outs = []
    for g, w in enumerate(POOL_WINDOWS):
        c0, c1 = g * POOL_GROUP_W, (g + 1) * POOL_GROUP_W
        ssum = cs[:, hi:hi + T, c0:c1] - cs[:, hi - w:hi - w + T, c0:c1]
        n = cnt[hi:hi + T] - cnt[hi - w:hi - w + T]
        pooled = ssum / n[None, :, None] - ext[:, POOL_BUF:, c0:c1]
        outs.append(jnp.einsum('btc,cd->btd', pooled, w_pool_l[g].astype(jnp.float32)))
    y = jnp.concatenate(outs, axis=-1) * scale_l.astype(jnp.float32)
    return y.astype(u.dtype), ext[:, -POOL_BUF:].astype(buf.dtype)


def mixer_block(x, start, k_past, v_past, s_ret, buf, norm_g, w_in, qn, kn, sb_b, ret_g, w_pool_l, pool_scale_l, w_out):
    B, T = x.shape[0], x.shape[1]
    h = rms_norm(x, norm_g)
    p = jnp.einsum('btd,de->bte', h, w_in)
    offs = [W_SB, 2 * W_SB, 3 * W_SB, 3 * W_SB + W_RET, 3 * W_SB + 2 * W_RET,
            3 * W_SB + 3 * W_RET, 3 * W_SB + 4 * W_RET]
    qa, ka, va, qr, kr, vr, gr, u = jnp.split(p, offs, axis=-1)
    qa = rms_norm(qa.reshape(B, T, H_SB, HEAD_DIM), qn)
    ka = rms_norm(ka.reshape(B, T, H_SB, HEAD_DIM), kn)
    va = va.reshape(B, T, H_SB, HEAD_DIM)
    if k_past is None:
        k_all, v_all = ka, va
    else:
        k_all = jnp.concatenate([k_past.astype(ka.dtype), ka], axis=1)
        v_all = jnp.concatenate([v_past.astype(va.dtype), va], axis=1)
    o_sb = stick_breaking(qa, k_all, v_all, start, sb_b).reshape(B, T, W_SB).astype(x.dtype)

    pos = start + jnp.arange(T)
    qr = rope(qr.reshape(B, T, H_RET, HEAD_DIM), pos)
    kr = rope(kr.reshape(B, T, H_RET, HEAD_DIM), pos) * (HEAD_DIM ** -0.5)
    o_r, s_new = retention(qr, kr, vr.reshape(B, T, H_RET, HEAD_DIM), s_ret)
    o_r = rms_norm(o_r, ret_g).reshape(B, T, W_RET)
    o_r = (jax.nn.silu(gr.astype(jnp.float32)) * o_r).astype(x.dtype)

    o_p, buf_new = pool_mixer(u, buf, start, w_pool_l, pool_scale_l)
    mix = jnp.einsum('btm,md->btd', jnp.concatenate([o_sb, o_r, o_p], axis=-1), w_out)
    return x + mix, ka, va, s_new.astype(s_ret.dtype), buf_new


def swiglu(h, wg, wu, wd):
    return jnp.einsum('btf,fd->btd', jax.nn.silu(jnp.einsum('btd,df->btf', h, wg)) * jnp.einsum('btd,df->btf', h, wu), wd)


def moe(h, router, wg, wu, wd):
    logits = jnp.einsum('btd,de->bte', h, router, preferred_element_type=jnp.float32)
    topv, topi = lax.top_k(logits, TOP_K)
    gates = jax.nn.softmax(topv, axis=-1)
    combine = jnp.sum(jax.nn.one_hot(topi, N_EXPERTS, dtype=jnp.float32) * gates[..., None], axis=-2)
    y = jnp.zeros(h.shape, jnp.float32)
    for e in range(N_EXPERTS):
        y = y + combine[..., e:e + 1] * swiglu(h, wg[e], wu[e], wd[e]).astype(jnp.float32)
    return y.astype(h.dtype)


def channel_mixer(x, l, norm_ffn, w_gate, w_up, w_down, router, w_gate_exp, w_up_exp, w_down_exp):
    h = rms_norm(x, norm_ffn[l])
    i = l // 2
    if l % 2 == 0:
        y = swiglu(h, w_gate[i], w_up[i], w_down[i])
    else:
        y = moe(h, router[i], w_gate_exp[i], w_up_exp[i], w_down_exp[i])
    return x + y


def setup_inputs(seed: int = 0) -> dict:
    key = jax.random.key(seed)
    ks = jax.random.split(key, 24)
    n_pages = PAST_LEN // PAGE_SIZE
    n_used = DEC_BATCH * n_pages
    n_pool = n_used + n_used // 4

    def nrm(k, shape, s):
        return jax.random.normal(k, shape, jnp.float32) * s

    page_table = jax.random.permutation(ks[6], n_pool)[:n_used].astype(jnp.int32).reshape(DEC_BATCH, n_pages)
    sb_bias = jnp.linspace(SB_BIAS_HI, SB_BIAS_LO, H_SB, dtype=jnp.float32)[None, :] + nrm(ks[23], (DEPTH, H_SB), 0.05)
    return {
        "x_prompt": nrm(ks[0], (BATCH, SEQ, D_MODEL), 1.0),
        "x_sample": nrm(ks[1], (DEC_BATCH, DEC_SEQ, D_MODEL), 1.0),
        "cache_k": nrm(ks[2], (DEPTH, n_pool, PAGE_SIZE, H_SB, HEAD_DIM), 1.0),
        "cache_v": nrm(ks[3], (DEPTH, n_pool, PAGE_SIZE, H_SB, HEAD_DIM), 1.0),
        "state_ret": nrm(ks[4], (DEPTH, DEC_BATCH, H_RET, HEAD_DIM, HEAD_DIM), 0.5),
        "state_pool": nrm(ks[5], (DEPTH, DEC_BATCH, POOL_BUF, W_POOL), 1.0),
        "page_table": page_table,
        "norm_mix": 1.0 + nrm(ks[7], (DEPTH, D_MODEL), 0.02),
        "w_in": nrm(ks[8], (DEPTH, D_MODEL, IN_W), D_MODEL ** -0.5),
        "q_norm": 1.0 + nrm(ks[9], (DEPTH, HEAD_DIM), 0.02),
        "k_norm": 1.0 + nrm(ks[10], (DEPTH, HEAD_DIM), 0.02),
        "sb_bias": sb_bias,
        "ret_norm": 1.0 + nrm(ks[11], (DEPTH, H_RET, HEAD_DIM), 0.02),
        "w_pool": nrm(ks[12], (DEPTH, N_POOL_GROUPS, POOL_GROUP_W, POOL_GROUP_W), POOL_GROUP_W ** -0.5),
        "pool_scale": 0.5 + nrm(ks[13], (DEPTH, W_POOL), 0.1),
        "w_out": nrm(ks[14], (DEPTH, MIX_W, D_MODEL), MIX_W ** -0.5),
        "norm_ffn": 1.0 + nrm(ks[15], (DEPTH, D_MODEL), 0.02),
        "w_gate": nrm(ks[16], (N_DENSE, D_MODEL, D_FF), D_MODEL ** -0.5),
        "w_up": nrm(ks[17], (N_DENSE, D_MODEL, D_FF), D_MODEL ** -0.5),
        "w_down": nrm(ks[18], (N_DENSE, D_FF, D_MODEL), D_FF ** -0.5),
        "router": nrm(ks[19], (N_MOE, D_MODEL, N_EXPERTS), D_MODEL ** -0.5),
        "w_gate_exp": nrm(ks[20], (N_MOE, N_EXPERTS, D_MODEL, D_FF_EXPERT), D_MODEL ** -0.5),
        "w_up_exp": nrm(ks[21], (N_MOE, N_EXPERTS, D_MODEL, D_FF_EXPERT), D_MODEL ** -0.5),
        "w_down_exp": nrm(ks[22], (N_MOE, N_EXPERTS, D_FF_EXPERT, D_MODEL), D_FF_EXPERT ** -0.5),
    }


def reference(x_prompt, x_sample, cache_k, cache_v, state_ret, state_pool, page_table,
              norm_mix, w_in, q_norm, k_norm, sb_bias, ret_norm, w_pool, pool_scale, w_out,
              norm_ffn, w_gate, w_up, w_down, router, w_gate_exp, w_up_exp, w_down_exp):
    db, n_pages = page_table.shape
    past_len = n_pages * PAGE_SIZE
    bp = x_prompt.shape[0]
    xp, xs = x_prompt, x_sample
    kp_l, vp_l, sp_l, bp_l, ks_l, vs_l, ss_l, bs_l = [], [], [], [], [], [], [], []
    for l in range(DEPTH):
        layer_w = (norm_mix[l], w_in[l], q_norm[l], k_norm[l], sb_bias[l], ret_norm[l], w_pool[l], pool_scale[l], w_out[l])
        s0 = jnp.zeros((bp, H_RET, HEAD_DIM, HEAD_DIM), state_ret.dtype)
        buf0 = jnp.zeros((bp, POOL_BUF, W_POOL), state_pool.dtype)
        xp, kp, vp, sp, bfp = mixer_block(xp, 0, None, None, s0, buf0, *layer_w)
        k_past = cache_k[l][page_table].reshape(db, past_len, H_SB, HEAD_DIM)
        v_past = cache_v[l][page_table].reshape(db, past_len, H_SB, HEAD_DIM)
        xs, ks_, vs_, ss_, bfs = mixer_block(xs, past_len, k_past, v_past, state_ret[l], state_pool[l], *layer_w)
        xp = channel_mixer(xp, l, norm_ffn, w_gate, w_up, w_down, router, w_gate_exp, w_up_exp, w_down_exp)
        xs = channel_mixer(xs, l, norm_ffn, w_gate, w_up, w_down, router, w_gate_exp, w_up_exp, w_down_exp)
        kp_l.append(kp); vp_l.append(vp); sp_l.append(sp); bp_l.append(bfp)
        ks_l.append(ks_); vs_l.append(vs_); ss_l.append(ss_); bs_l.append(bfs)
    return (xp, xs,
            jnp.stack(kp_l), jnp.stack(vp_l), jnp.stack(sp_l), jnp.stack(bp_l),
            jnp.stack(ks_l), jnp.stack(vs_l), jnp.stack(ss_l), jnp.stack(bs_l))
```

```python
import functools

import jax
import jax.numpy as jnp
from jax import lax
from jax.experimental import pallas as pl
from jax.experimental.pallas import tpu as pltpu

F32 = jnp.float32
BF16 = jnp.bfloat16

HEAD_DIM = 64
LANES = 128
PAGE_SIZE = 128
POOL_WINDOWS = (2, 4, 8, 16)
POOL_BUF = max(POOL_WINDOWS) - 1
POOL_PAD = POOL_BUF + 1
RET_CHUNK = 128
ROPE_THETA = 10000.0
TOP_K = 2
EPS = 1e-6
VMEM_LIMIT = 52 * 1024 * 1024


def _params(*sem):
    return pltpu.CompilerParams(dimension_semantics=sem, vmem_limit_bytes=VMEM_LIMIT)


def _row_tile(n, target, mult=16):
    best = None
    for t in range(mult, min(n, target) + 1, mult):
        if n % t == 0:
            best = t
    return best if best is not None else n


def _split_bf16(x):
    hi = x.astype(BF16)
    lo = (x - hi.astype(F32)).astype(BF16)
    return hi, lo


def _group_ones():
    r = lax.broadcasted_iota(jnp.int32, (LANES, LANES), 0) // HEAD_DIM
    c = lax.broadcasted_iota(jnp.int32, (LANES, LANES), 1) // HEAD_DIM
    return (r == c).astype(BF16)


def _head_sumsq(x, ones_bd):
    hi, lo = _split_bf16(x * x)
    return (jnp.dot(hi, ones_bd, preferred_element_type=F32)
            + jnp.dot(lo, ones_bd, preferred_element_type=F32))


def _dot_nt(a, b):
    return lax.dot_general(a, b, (((1,), (1,)), ((), ())), preferred_element_type=F32)


def _rms_kernel(x_ref, g_ref, o_ref):
    x = x_ref[...]
    ms = jnp.mean(x * x, axis=-1, keepdims=True)
    o_ref[...] = (x * lax.rsqrt(ms + EPS) * g_ref[0]).astype(o_ref.dtype)


def _rmsnorm(x, g_all, layer):
    n, d = x.shape
    tm = _row_tile(n, 640)
    return pl.pallas_call(
        _rms_kernel,
        grid=(n // tm,),
        in_specs=[pl.BlockSpec((tm, d), lambda i: (i, 0)),
                  pl.BlockSpec((1, 1, d), lambda i: (layer, 0, 0))],
        out_specs=pl.BlockSpec((tm, d), lambda i: (i, 0)),
        out_shape=jax.ShapeDtypeStruct((n, d), BF16),
        compiler_params=_params("parallel"),
        name="rmsnorm",
    )(x, g_all.reshape(g_all.shape[0], 1, d))


def _mm_kernel(a_ref, w_ref, *rest, nk, has_res):
    if has_res:
        r_ref, o_ref = rest
    else:
        (o_ref,) = rest
    prod = jnp.dot(a_ref[...], w_ref[...].astype(BF16), preferred_element_type=F32)
    if nk == 1:
        o_ref[...] = prod + r_ref[...] if has_res else prod
        return
    k = pl.program_id(2)

    @pl.when(k == 0)
    def _():
        o_ref[...] = prod + r_ref[...] if has_res else prod

    @pl.when(k > 0)
    def _():
        o_ref[...] += prod


def _matmul(a, w_all, layer, res=None, *, tm, tn, tk):
    m, kd = a.shape
    nd = w_all.shape[-1]
    nk = kd // tk
    in_specs = [pl.BlockSpec((tm, tk), lambda i, j, k: (i, k)),
                pl.BlockSpec((None, tk, tn), lambda i, j, k: (layer, k, j))]
    args = [a, w_all]
    if res is not None:
        in_specs.append(pl.BlockSpec((tm, tn), lambda i, j, k: (i, j)))
        args.append(res)
    return pl.pallas_call(
        functools.partial(_mm_kernel, nk=nk, has_res=res is not None),
        grid=(m // tm, nd // tn, nk),
        in_specs=in_specs,
        out_specs=pl.BlockSpec((tm, tn), lambda i, j, k: (i, j)),
        out_shape=jax.ShapeDtypeStruct((m, nd), F32),
        compiler_params=_params("parallel", "parallel", "arbitrary"),
        name="matmul",
    )(*args)


def _gateup_kernel(te_ref, nv_ref, a_ref, wg_ref, wu_ref, o_ref):
    i = pl.program_id(1)

    @pl.when(i < nv_ref[0])
    def _():
        a = a_ref[...]
        g = jnp.dot(a, wg_ref[...].astype(BF16), preferred_element_type=F32)
        u = jnp.dot(a, wu_ref[...].astype(BF16), preferred_element_type=F32)
        o_ref[...] = (g * jax.nn.sigmoid(g) * u).astype(o_ref.dtype)

    @pl.when(i >= nv_ref[0])
    def _():
        o_ref[...] = jnp.zeros_like(o_ref)


def _gateup(a, wg_all, wu_all, layer, tile_expert, n_valid, *, tm, tf):
    r, kd = a.shape
    fd = wg_all.shape[-1]
    w_spec = pl.BlockSpec((None, None, kd, tf), lambda j, i, te, nv: (layer, te[i], 0, j))
    return pl.pallas_call(
        _gateup_kernel,
        grid_spec=pltpu.PrefetchScalarGridSpec(
            num_scalar_prefetch=2,
            grid=(fd // tf, r // tm),
            in_specs=[pl.BlockSpec((tm, kd), lambda j, i, te, nv: (jnp.minimum(i, nv[0] - 1), 0)),
                      w_spec, w_spec],
            out_specs=pl.BlockSpec((tm, tf), lambda j, i, te, nv: (i, j)),
        ),
        out_shape=jax.ShapeDtypeStruct((r, fd), BF16),
        compiler_params=_params("parallel", "arbitrary"),
        name="gateup",
    )(tile_expert, n_valid, a, wg_all, wu_all)


def _down_kernel(te_ref, nv_ref, a_ref, w_ref, s_ref, o_ref, *, nk):
    i = pl.program_id(0)
    k = pl.program_id(1)

    @pl.when(i < nv_ref[0])
    def _():
        prod = jnp.dot(a_ref[...], w_ref[...].astype(BF16), preferred_element_type=F32)

        @pl.when(k == 0)
        def _():
            o_ref[...] = prod

        @pl.when(k > 0)
        def _():
            o_ref[...] += prod

        @pl.when(k == nk - 1)
        def _():
            o_ref[...] = o_ref[...] * s_ref[...]

    @pl.when(jnp.logical_and(i >= nv_ref[0], k == 0))
    def _():
        o_ref[...] = jnp.zeros_like(o_ref)


def _down(a, wd_all, layer, tile_expert, n_valid, scale, *, tm, tk):
    r, fd = a.shape
    dd = wd_all.shape[-1]
    nk = fd // tk

    def row(i, nv):
        return jnp.minimum(i, nv[0] - 1)

    return pl.pallas_call(
        functools.partial(_down_kernel, nk=nk),
        grid_spec=pltpu.PrefetchScalarGridSpec(
            num_scalar_prefetch=2,
            grid=(r // tm, nk),
            in_specs=[pl.BlockSpec((tm, tk), lambda i, k, te, nv: (row(i, nv), k)),
                      pl.BlockSpec((None, None, tk, dd), lambda i, k, te, nv: (layer, te[i], k, 0)),
                      pl.BlockSpec((tm, 1), lambda i, k, te, nv: (i, 0))],
            out_specs=pl.BlockSpec((tm, dd), lambda i, k, te, nv: (i, 0)),
        ),
        out_shape=jax.ShapeDtypeStruct((r, dd), F32),
        compiler_params=_params("parallel", "arbitrary"),
        name="expert_down",
    )(tile_expert, n_valid, a, wd_all, scale)


def _router_kernel(x_ref, g_ref, r_ref, o_ref, *, n_experts):
    x = x_ref[...]
    ms = jnp.mean(x * x, axis=-1, keepdims=True)
    h = x * lax.rsqrt(ms + EPS) * g_ref[0]
    h_hi, h_lo = _split_bf16(h)
    r_hi, r_lo = _split_bf16(r_ref[0])
    logits = (jnp.dot(h_hi, r_hi, preferred_element_type=F32)
              + jnp.dot(h_hi, r_lo, preferred_element_type=F32)
              + jnp.dot(h_lo, r_hi, preferred_element_type=F32))
    lane = lax.broadcasted_iota(jnp.int32, logits.shape, 1)
    neg = jnp.float32(-jnp.inf)
    big = jnp.int32(LANES)
    logits = jnp.where(lane < n_experts, logits, neg)
    m1 = jnp.max(logits, axis=-1, keepdims=True)
    i1 = jnp.min(jnp.where(logits == m1, lane, big), axis=-1, keepdims=True)
    rest = jnp.where(lane == i1, neg, logits)
    m2 = jnp.max(rest, axis=-1, keepdims=True)
    i2 = jnp.min(jnp.where(rest == m2, lane, big), axis=-1, keepdims=True)
    e2 = jnp.exp(m2 - m1)
    den = 1.0 + e2
    g1 = 1.0 / den
    g2 = e2 / den
    out = jnp.where(lane == 0, i1.astype(F32),
                    jnp.where(lane == 1, i2.astype(F32),
                              jnp.where(lane == 2, g1, jnp.where(lane == 3, g2, 0.0))))
    o_ref[...] = out


def _router(x, g_all, layer, router_pad, n_experts):
    n, d = x.shape
    tm = _row_tile(n, 640)
    return pl.pallas_call(
        functools.partial(_router_kernel, n_experts=n_experts),
        grid=(n // tm,),
        in_specs=[pl.BlockSpec((tm, d), lambda i: (i, 0)),
                  pl.BlockSpec((1, 1, d), lambda i: (layer, 0, 0)),
                  pl.BlockSpec((1, d, LANES), lambda i: (0, 0, 0))],
        out_specs=pl.BlockSpec((tm, LANES), lambda i: (i, 0)),
        out_shape=jax.ShapeDtypeStruct((n, LANES), F32),
        compiler_params=_params("parallel"),
        name="router",
    )(x, g_all.reshape(g_all.shape[0], 1, d), router_pad)


def _gather_norm_kernel(nv_ref, idx_ref, x_hbm, g_ref, o_ref, buf, sem, *, tg):
    i = pl.program_id(0)

    @pl.when(i < nv_ref[0])
    def _():
        def issue(r, c):
            pltpu.make_async_copy(x_hbm.at[pl.ds(idx_ref[0, 0, r], 1)], buf.at[pl.ds(r, 1)], sem).start()
            return c

        lax.fori_loop(0, tg, issue, 0)
        pltpu.make_async_copy(x_hbm.at[pl.ds(0, tg)], buf, sem).wait()
        x = buf[...]
        ms = jnp.mean(x * x, axis=-1, keepdims=True)
        o_ref[...] = (x * lax.rsqrt(ms + EPS) * g_ref[0]).astype(o_ref.dtype)

    @pl.when(i >= nv_ref[0])
    def _():
        o_ref[...] = jnp.zeros_like(o_ref)


def _gather_norm(x, g_all, layer, src_row, n_valid_tiles, *, tg):
    n, d = x.shape
    r = src_row.shape[0]
    nt = r // tg
    return pl.pallas_call(
        functools.partial(_gather_norm_kernel, tg=tg),
        grid_spec=pltpu.PrefetchScalarGridSpec(
            num_scalar_prefetch=1,
            grid=(nt,),
            in_specs=[pl.BlockSpec((1, 1, tg), lambda i, nv: (i, 0, 0), memory_space=pltpu.SMEM),
                      pl.BlockSpec(memory_space=pl.ANY),
                      pl.BlockSpec((1, 1, d), lambda i, nv: (layer, 0, 0))],
            out_specs=pl.BlockSpec((tg, d), lambda i, nv: (i, 0)),
            scratch_shapes=[pltpu.VMEM((tg, d), F32), pltpu.SemaphoreType.DMA(())],
        ),
        out_shape=jax.ShapeDtypeStruct((r, d), BF16),
        compiler_params=_params("arbitrary"),
        name="gather_norm",
    )(n_valid_tiles, src_row.reshape(nt, 1, tg), x, g_all.reshape(g_all.shape[0], 1, d))


def _combine_kernel(p1_ref, p2_ref, x_ref, d_hbm, o_ref, buf1, buf2, sem, *, tc):
    def issue(r, c):
        pltpu.make_async_copy(d_hbm.at[pl.ds(p1_ref[0, 0, r], 1)], buf1.at[pl.ds(r, 1)], sem.at[0]).start()
        pltpu.make_async_copy(d_hbm.at[pl.ds(p2_ref[0, 0, r], 1)], buf2.at[pl.ds(r, 1)], sem.at[1]).start()
        return c

    lax.fori_loop(0, tc, issue, 0)
    pltpu.make_async_copy(d_hbm.at[pl.ds(0, tc)], buf1, sem.at[0]).wait()
    pltpu.make_async_copy(d_hbm.at[pl.ds(0, tc)], buf2, sem.at[1]).wait()
    o_ref[...] = x_ref[...] + (buf1[...] + buf2[...])


def _combine(x, dsorted, pos1, pos2, *, tc):
    n, d = x.shape
    nt = n // tc
    idx_spec = pl.BlockSpec((1, 1, tc), lambda i: (i, 0, 0), memory_space=pltpu.SMEM)
    return pl.pallas_call(
        functools.partial(_combine_kernel, tc=tc),
        grid=(nt,),
        in_specs=[idx_spec, idx_spec,
                  pl.BlockSpec((tc, d), lambda i: (i, 0)),
                  pl.BlockSpec(memory_space=pl.ANY)],
        out_specs=pl.BlockSpec((tc, d), lambda i: (i, 0)),
        out_shape=jax.ShapeDtypeStruct((n, d), F32),
        scratch_shapes=[pltpu.VMEM((tc, d), F32), pltpu.VMEM((tc, d), F32),
                        pltpu.SemaphoreType.DMA((2,))],
        compiler_params=_params("arbitrary"),
        name="combine",
    )(pos1.reshape(nt, 1, tc), pos2.reshape(nt, 1, tc), x, dsorted)


def _sb_prep_kernel(q_ref, k_ref, v_ref, qg_ref, kg_ref, qb_ref, kf_ref, kb_ref, vb_ref, *, scale):
    ones_bd = _group_ones()
    w = q_ref.shape[1]
    for c in range(w // LANES):
        sl = slice(c * LANES, (c + 1) * LANES)
        q = q_ref[:, sl]
        qn = q * lax.rsqrt(_head_sumsq(q, ones_bd) * (1.0 / HEAD_DIM) + EPS) * qg_ref[:, sl]
        qb_ref[:, sl] = (qn * scale).astype(BF16)
        k = k_ref[:, sl]
        kn = k * lax.rsqrt(_head_sumsq(k, ones_bd) * (1.0 / HEAD_DIM) + EPS) * kg_ref[:, sl]
        kf_ref[:, sl] = kn
        kb_ref[:, sl] = kn.astype(BF16)
    vb_ref[...] = v_ref[...].astype(BF16)


def _sb_prep(p, qg, kg, w_sb):
    n = p.shape[0]
    tm = _row_tile(n, 640)
    blk = lambda c: pl.BlockSpec((tm, w_sb), lambda i: (i, c))
    gain = pl.BlockSpec((1, w_sb), lambda i: (0, 0))
    out = pl.BlockSpec((tm, w_sb), lambda i: (i, 0))
    return pl.pallas_call(
        functools.partial(_sb_prep_kernel, scale=HEAD_DIM ** -0.5),
        grid=(n // tm,),
        in_specs=[blk(0), blk(1), blk(2), gain, gain],
        out_specs=[out, out, out, out],
        out_shape=[jax.ShapeDtypeStruct((n, w_sb), BF16), jax.ShapeDtypeStruct((n, w_sb), F32),
                   jax.ShapeDtypeStruct((n, w_sb), BF16), jax.ShapeDtypeStruct((n, w_sb), BF16)],
        compiler_params=_params("parallel"),
        name="sb_prep",
    )(p, p, p, qg, kg)


def _neg_softplus(z):
    return -(jnp.maximum(z, 0.0) + jnp.log(1.0 + jnp.exp(-jnp.abs(z))))


def _suffix_matrix(tk):
    r = lax.broadcasted_iota(jnp.int32, (tk, tk + LANES), 0)
    c = lax.broadcasted_iota(jnp.int32, (tk, tk + LANES), 1)
    return jnp.logical_or(r > c, c >= tk).astype(BF16)


def _sb_block(z, carry, wmat, mask):
    ls = _neg_softplus(z)
    if mask is not None:
        ls = jnp.where(mask, ls, 0.0)
    hi, lo = _split_bf16(ls)
    t = jnp.dot(hi, wmat, preferred_element_type=F32) + jnp.dot(lo, wmat, preferred_element_type=F32)
    tail = t[:, :LANES] + carry
    a = jnp.exp(z + ls + tail)
    if mask is not None:
        a = jnp.where(mask, a, 0.0)
    return a, carry + t[:, LANES:]


def _sb_prompt_kernel(bias_ref, q_ref, k_ref, v_ref, o_ref, acc_ref, car_ref, *, tq):
    tk = LANES
    hp = pl.program_id(1)
    qb = pl.program_id(2)
    q = q_ref[...]
    lane = lax.broadcasted_iota(jnp.int32, (tq, LANES), 1)
    first = lane < HEAD_DIM
    zero = jnp.zeros_like(q)
    qh = (jnp.where(first, q, zero), jnp.where(first, zero, q))
    bh = (bias_ref[2 * hp], bias_ref[2 * hp + 1])
    wmat = _suffix_matrix(tk)
    acc_ref[...] = jnp.zeros_like(acc_ref)
    car_ref[...] = jnp.zeros_like(car_ref)
    row = lax.broadcasted_iota(jnp.int32, (tq, tk), 0)
    col = lax.broadcasted_iota(jnp.int32, (tq, tk), 1)

    def block(k0, mask):
        kblk = k_ref[pl.ds(k0, tk), :]
        vblk = v_ref[pl.ds(k0, tk), :]
        for h in range(2):
            z = _dot_nt(qh[h], kblk) + bh[h]
            a, car = _sb_block(z, car_ref[h], wmat, mask)
            car_ref[h] = car
            acc_ref[h] += jnp.dot(a.astype(BF16), vblk, preferred_element_type=F32)

    for d in range(tq // tk - 1, -1, -1):
        block(pl.multiple_of(qb * tq + d * tk, tk), (col + d * tk) < row)

    def full(j, c):
        k0 = pl.multiple_of(qb * tq - (j + 1) * tk, tk)
        block(k0, None)
        return c

    lax.fori_loop(0, qb * (tq // tk), full, 0)
    o_ref[...] = jnp.where(first, acc_ref[0], acc_ref[1]).astype(o_ref.dtype)


def _sb_prompt(qb, kb, vb, bias, n_batch, seq, *, tq):
    w = qb.shape[1]
    nq = seq // tq
    return pl.pallas_call(
        functools.partial(_sb_prompt_kernel, tq=tq),
        grid_spec=pltpu.PrefetchScalarGridSpec(
            num_scalar_prefetch=1,
            grid=(n_batch, w // LANES, nq),
            in_specs=[pl.BlockSpec((tq, LANES), lambda b, h, i, bias: (b * nq + i, h)),
                      pl.BlockSpec((seq, LANES), lambda b, h, i, bias: (b, h)),
                      pl.BlockSpec((seq, LANES), lambda b, h, i, bias: (b, h))],
            out_specs=pl.BlockSpec((tq, LANES), lambda b, h, i, bias: (b * nq + i, h)),
            scratch_shapes=[pltpu.VMEM((2, tq, LANES), F32), pltpu.VMEM((2, tq, LANES), F32)],
        ),
        out_shape=jax.ShapeDtypeStruct((n_batch * seq, w), BF16),
        compiler_params=_params("parallel", "parallel", "arbitrary"),
        name="sb_prompt",
    )(bias, qb, kb, vb)


def _sb_sample_kernel(pt_ref, q_ref, bias_ref, kn_ref, vn_ref, *rest, n_heads, t_new, pages_per_step):
    page_refs = rest[:2 * pages_per_step]
    o_ref, acc_ref, car_ref = rest[2 * pages_per_step:]
    s = pl.program_id(1)
    n_pairs = n_heads // 2
    rows = n_heads * t_new
    wmat = _suffix_matrix(PAGE_SIZE)
    bias = bias_ref[...]

    def page(k_ref, v_ref, mask):
        zs = []
        vps = []
        for hp in range(n_pairs):
            kp = jnp.concatenate([k_ref[:, 2 * hp, :], k_ref[:, 2 * hp + 1, :]], axis=1).astype(BF16)
            vps.append(jnp.concatenate([v_ref[:, 2 * hp, :], v_ref[:, 2 * hp + 1, :]], axis=1).astype(BF16))
            zs.append(_dot_nt(q_ref[0, hp * 2 * t_new:(hp + 1) * 2 * t_new, :], kp))
        z = jnp.concatenate(zs, axis=0) + bias
        a, car = _sb_block(z, car_ref[...], wmat, mask)
        car_ref[...] = car
        a = a.astype(BF16)
        for hp in range(n_pairs):
            sl = slice(hp * 2 * t_new, (hp + 1) * 2 * t_new)
            acc_ref[sl, :] += jnp.dot(a[sl, :], vps[hp], preferred_element_type=F32)

    @pl.when(s == 0)
    def _():
        acc_ref[...] = jnp.zeros_like(acc_ref)
        car_ref[...] = jnp.zeros_like(car_ref)
        t_idx = lax.broadcasted_iota(jnp.int32, (rows, PAGE_SIZE), 0) % t_new
        j_idx = lax.broadcasted_iota(jnp.int32, (rows, PAGE_SIZE), 1)
        page(kn_ref.at[0], vn_ref.at[0], j_idx < t_idx)

    for j in range(pages_per_step):
        page(page_refs[2 * j].at[0, 0], page_refs[2 * j + 1].at[0, 0], None)

    @pl.when(s == pl.num_programs(1) - 1)
    def _():
        lane = lax.broadcasted_iota(jnp.int32, (t_new, LANES), 1)
        for hp in range(n_pairs):
            base = hp * 2 * t_new
            o_ref[0, :, hp * LANES:(hp + 1) * LANES] = jnp.where(
                lane < HEAD_DIM, acc_ref[base:base + t_new, :],
                acc_ref[base + t_new:base + 2 * t_new, :]).astype(o_ref.dtype)


def _sb_sample(q_pairs, bias_rows, k_new, v_new, cache_k, cache_v, layer, page_table, *, pages_per_step):
    nb, n_pages = page_table.shape
    rows = q_pairs.shape[1]
    n_heads = cache_k.shape[3]
    t_new = rows // n_heads
    steps = n_pages // pages_per_step
    pt = page_table.reshape(-1)

    def page_spec(j):
        def imap(b, s, pt_ref):
            return (layer, pt_ref[b * n_pages + n_pages - 1 - (s * pages_per_step + j)], 0, 0, 0)
        return pl.BlockSpec((1, 1, PAGE_SIZE, n_heads, HEAD_DIM), imap)

    page_specs = []
    page_args = []
    for j in range(pages_per_step):
        page_specs += [page_spec(j), page_spec(j)]
        page_args += [cache_k, cache_v]
    new_spec = pl.BlockSpec((1, PAGE_SIZE, n_heads, HEAD_DIM), lambda b, s, pt_ref: (b, 0, 0, 0))
    return pl.pallas_call(
        functools.partial(_sb_sample_kernel, n_heads=n_heads, t_new=t_new, pages_per_step=pages_per_step),
        grid_spec=pltpu.PrefetchScalarGridSpec(
            num_scalar_prefetch=1,
            grid=(nb, steps),
            in_specs=[pl.BlockSpec((1, rows, LANES), lambda b, s, pt_ref: (b, 0, 0)),
                      pl.BlockSpec((rows, LANES), lambda b, s, pt_ref: (0, 0)),
                      new_spec, new_spec] + page_specs,
            out_specs=pl.BlockSpec((1, t_new, n_heads * HEAD_DIM), lambda b, s, pt_ref: (b, 0, 0)),
            scratch_shapes=[pltpu.VMEM((rows, LANES), F32), pltpu.VMEM((rows, LANES), F32)],
        ),
        out_shape=jax.ShapeDtypeStruct((nb, t_new, n_heads * HEAD_DIM), F32),
        compiler_params=_params("parallel", "arbitrary"),
        name="sb_sample",
    )(pt, q_pairs, bias_rows, k_new, v_new, *page_args)


def _rope(x, cos, sin_signed, first_half):
    partner = jnp.where(first_half, pltpu.roll(x, LANES - HEAD_DIM // 2, 1), pltpu.roll(x, HEAD_DIM // 2, 1))
    return x * cos + partner * sin_signed


def _ret_kernel(q_ref, k_ref, v_ref, g_ref, cos_ref, sin_ref, dec_ref, qd_ref, kd_ref, cd_ref, s0_ref, rg_ref,
                o_ref, s_ref, st_ref):
    c = pl.program_id(2)
    rows = q_ref.shape[0]
    lane = lax.broadcasted_iota(jnp.int32, (rows, LANES), 1)
    first = lane < HEAD_DIM
    first_half = (lane % HEAD_DIM) < HEAD_DIM // 2

    @pl.when(c == 0)
    def _():
        st_ref[...] = s0_ref[0, 0]

    cos = cos_ref[...]
    sin = sin_ref[...]
    q = _rope(q_ref[...], cos, sin, first_half)
    k = _rope(k_ref[...], cos, sin, first_half) * (HEAD_DIM ** -0.5)
    qb = q.astype(BF16)
    kb = k.astype(BF16)
    vb = v_ref[...].astype(BF16)
    zero = jnp.zeros_like(qb)
    state = st_ref[...]
    o = jnp.dot(qb, state.astype(BF16), preferred_element_type=F32) * qd_ref[0]
    outs = []
    for h in range(2):
        qh = jnp.where(first, qb, zero) if h == 0 else jnp.where(first, zero, qb)
        att = _dot_nt(qh, kb) * dec_ref[0, h]
        outs.append(jnp.dot(att.astype(BF16), vb, preferred_element_type=F32))
    o = o + jnp.where(first, outs[0], outs[1])
    kd_t = jnp.transpose(k * kd_ref[0]).astype(BF16)
    upd = jnp.dot(kd_t, vb, preferred_element_type=F32)
    r_i = lax.broadcasted_iota(jnp.int32, (LANES, LANES), 0) // HEAD_DIM
    c_i = lax.broadcasted_iota(jnp.int32, (LANES, LANES), 1) // HEAD_DIM
    new_state = state * cd_ref[0] + jnp.where(r_i == c_i, upd, 0.0)
    st_ref[...] = new_state
    s_ref[0, 0] = new_state

    y = o * lax.rsqrt(_head_sumsq(o, _group_ones()) * (1.0 / HEAD_DIM) + EPS) * rg_ref[...]
    g = g_ref[...]
    o_ref[...] = (g * jax.nn.sigmoid(g) * y).astype(o_ref.dtype)


def _retention(p, row0, n_batch, seq, chunk, col0, w_ret, tabs, s0_bd, ret_gain):
    cos, sin, dec, qd, kd, cd = tabs
    n_pairs = w_ret // LANES
    nc = seq // chunk
    rb0 = row0 // chunk
    cb0 = col0 // LANES

    def col(g):
        return pl.BlockSpec((chunk, LANES), lambda b, h, c: (rb0 + b * nc + c, cb0 + g * n_pairs + h))

    return pl.pallas_call(
        _ret_kernel,
        grid=(n_batch, n_pairs, nc),
        in_specs=[col(0), col(1), col(2), col(3),
                  pl.BlockSpec((chunk, LANES), lambda b, h, c: (c, 0)),
                  pl.BlockSpec((chunk, LANES), lambda b, h, c: (c, 0)),
                  pl.BlockSpec((1, 2, chunk, chunk), lambda b, h, c: (h, 0, 0, 0)),
                  pl.BlockSpec((1, chunk, LANES), lambda b, h, c: (h, 0, 0)),
                  pl.BlockSpec((1, chunk, LANES), lambda b, h, c: (h, 0, 0)),
                  pl.BlockSpec((1, LANES, LANES), lambda b, h, c: (h, 0, 0)),
                  pl.BlockSpec((1, 1, LANES, LANES), lambda b, h, c: (b, h, 0, 0)),
                  pl.BlockSpec((1, LANES), lambda b, h, c: (0, h))],
        out_specs=[pl.BlockSpec((chunk, LANES), lambda b, h, c: (b * nc + c, h)),
                   pl.BlockSpec((1, 1, LANES, LANES), lambda b, h, c: (b, h, 0, 0))],
        out_shape=[jax.ShapeDtypeStruct((n_batch * seq, w_ret), BF16),
                   jax.ShapeDtypeStruct((n_batch, n_pairs, LANES, LANES), F32)],
        scratch_shapes=[pltpu.VMEM((LANES, LANES), F32)],
        compiler_params=_params("parallel", "parallel", "arbitrary"),
        name="retention",
    )(p, p, p, p, cos, sin, dec, qd, kd, cd, s0_bd, ret_gain)


def _rope_tables(pos):
    half = HEAD_DIM // 2
    inv = ROPE_THETA ** (-jnp.arange(half, dtype=F32) / half)
    ang = pos.astype(F32)[:, None] * inv[None, :]
    cos = jnp.cos(ang)
    sin = jnp.sin(ang)
    cos_t = jnp.tile(cos, (1, 2 * LANES // HEAD_DIM))
    sin_t = jnp.tile(jnp.concatenate([-sin, sin], axis=1), (1, LANES // HEAD_DIM))
    return cos_t, sin_t


def _decay_tables(rows, c_eff, n_heads):
    log_g = jnp.log(1.0 - 2.0 ** (-5.0 - jnp.arange(n_heads, dtype=F32)))
    idx = jnp.arange(rows, dtype=F32)
    diff = idx[:, None] - idx[None, :]
    decay = jnp.where(diff >= 0, jnp.exp(log_g[:, None, None] * jnp.maximum(diff, 0.0)), 0.0)
    q_dec = jnp.exp(log_g[:, None] * (idx + 1.0))
    k_dec = jnp.exp(log_g[:, None] * (c_eff - 1.0 - idx))
    chunk_dec = jnp.exp(log_g * c_eff)
    n_pairs = n_heads // 2

    def lanes(t):
        return jnp.repeat(t.reshape(n_pairs, 2, rows).transpose(0, 2, 1), HEAD_DIM, axis=2)

    dec = decay.reshape(n_pairs, 2, rows, rows)
    cd = jnp.repeat(chunk_dec.reshape(n_pairs, 2), HEAD_DIM, axis=1)
    cd = jnp.broadcast_to(cd[:, :, None], (n_pairs, LANES, LANES))
    return dec, lanes(q_dec), lanes(k_dec), cd


def _state_to_bd(s):
    b, h = s.shape[0], s.shape[1]
    s = s.reshape(b, h // 2, 2, HEAD_DIM, HEAD_DIM)
    z = jnp.zeros_like(s[:, :, 0])
    top = jnp.concatenate([s[:, :, 0], z], axis=-1)
    bot = jnp.concatenate([z, s[:, :, 1]], axis=-1)
    return jnp.concatenate([top, bot], axis=-2)


def _state_from_bd(s):
    b, hp = s.shape[0], s.shape[1]
    a = s[:, :, :HEAD_DIM, :HEAD_DIM]
    d = s[:, :, HEAD_DIM:, HEAD_DIM:]
    return jnp.stack([a, d], axis=2).reshape(b, hp * 2, HEAD_DIM, HEAD_DIM)


def _pool_kernel(u_ref, buf_ref, w_ref, sc_ref, o_ref, nb_ref, ext_ref, raw_ref, *, start, seq):
    u = u_ref[...]
    hist = buf_ref[0]
    raw_ref[0:POOL_PAD, :] = hist
    raw_ref[POOL_PAD:, :] = u
    r = lax.broadcasted_iota(jnp.int32, hist.shape, 0)
    ext_ref[0:POOL_PAD, :] = jnp.where(start - POOL_PAD + r >= 0, hist, 0.0)
    ext_ref[POOL_PAD:, :] = u
    nb_ref[0] = raw_ref[seq:seq + POOL_PAD, :]
    t = lax.broadcasted_iota(jnp.int32, (seq, LANES), 0)
    for g, w in enumerate(POOL_WINDOWS):
        sl = slice(g * LANES, (g + 1) * LANES)
        ssum = ext_ref[POOL_PAD:POOL_PAD + seq, sl]
        for j in range(1, w):
            ssum = ssum + ext_ref[POOL_PAD - j:POOL_PAD - j + seq, sl]
        n = jnp.minimum(start + t + 1, w).astype(F32)
        pooled = ssum / n - u[:, sl]
        y = jnp.dot(pooled.astype(BF16), w_ref[0, g].astype(BF16), preferred_element_type=F32)
        o_ref[:, sl] = (y * sc_ref[0, :, sl]).astype(o_ref.dtype)


def _pool(p, row0, n_batch, seq, col0, w_pool_w, buf16, w_pool_all, scale_all, layer, start, out_dtype):
    rb0 = row0 // seq
    cb0 = col0 // w_pool_w
    return pl.pallas_call(
        functools.partial(_pool_kernel, start=start, seq=seq),
        grid=(n_batch,),
        in_specs=[pl.BlockSpec((seq, w_pool_w), lambda b: (rb0 + b, cb0)),
                  pl.BlockSpec((1, POOL_PAD, w_pool_w), lambda b: (b, 0, 0)),
                  pl.BlockSpec((1, len(POOL_WINDOWS), LANES, LANES), lambda b: (layer, 0, 0, 0)),
                  pl.BlockSpec((1, 1, w_pool_w), lambda b: (layer, 0, 0))],
        out_specs=[pl.BlockSpec((seq, w_pool_w), lambda b: (b, 0)),
                   pl.BlockSpec((1, POOL_PAD, w_pool_w), lambda b: (b, 0, 0))],
        out_shape=[jax.ShapeDtypeStruct((n_batch * seq, w_pool_w), out_dtype),
                   jax.ShapeDtypeStruct((n_batch, POOL_PAD, w_pool_w), F32)],
        scratch_shapes=[pltpu.VMEM((seq + POOL_PAD, w_pool_w), F32), pltpu.VMEM((seq + POOL_PAD, w_pool_w), F32)],
        compiler_params=_params("parallel"),
        name="pool",
    )(p, buf16, w_pool_all, scale_all.reshape(scale_all.shape[0], 1, w_pool_w))


def _routing(slab, n_experts, tm):
    n = slab.shape[0]
    experts = slab[:, :TOP_K].astype(jnp.int32).reshape(-1)
    gates = slab[:, TOP_K:2 * TOP_K].reshape(-1)
    onehot = (experts[:, None] == jnp.arange(n_experts, dtype=jnp.int32)[None, :]).astype(jnp.int32)
    csum = jnp.cumsum(onehot, axis=0)
    rank = jnp.take_along_axis(csum - onehot, experts[:, None], axis=1)[:, 0]
    counts = csum[-1]
    padded = ((counts + tm - 1) // tm) * tm
    ends = jnp.cumsum(padded)
    starts = ends - padded
    pos = starts[experts] + rank
    n_rows = -(-(TOP_K * n + n_experts * (tm - 1)) // tm) * tm
    n_tiles = n_rows // tm
    token = jnp.arange(TOP_K * n, dtype=jnp.int32) // TOP_K
    src_row = jnp.zeros((n_rows,), jnp.int32).at[pos].set(token)
    gate_sorted = jnp.zeros((n_rows,), F32).at[pos].set(gates)
    tile_start = jnp.arange(n_tiles, dtype=jnp.int32) * tm
    tile_expert = jnp.minimum(jnp.searchsorted(ends, tile_start, side="right"), n_experts - 1).astype(jnp.int32)
    n_valid = (ends[-1] // tm).astype(jnp.int32).reshape(1)
    pos = pos.reshape(n, TOP_K)
    return src_row, gate_sorted.reshape(n_rows, 1), tile_expert, n_valid, pos[:, 0], pos[:, 1]


def kernel(x_prompt, x_sample, cache_k, cache_v, state_ret, state_pool, page_table, norm_mix, w_in, q_norm,
           k_norm, sb_bias, ret_norm, w_pool, pool_scale, w_out, norm_ffn, w_gate, w_up, w_down, router,
           w_gate_exp, w_up_exp, w_down_exp):
    bp, seq, d = x_prompt.shape
    db, dseq, _ = x_sample.shape
    depth = w_in.shape[0]
    n_pages = page_table.shape[1]
    past_len = n_pages * PAGE_SIZE
    h_sb = cache_k.shape[3]
    w_sb = h_sb * HEAD_DIM
    h_ret = state_ret.shape[2]
    w_ret = h_ret * HEAD_DIM
    w_pl = state_pool.shape[3]
    n_experts = router.shape[-1]
    np_, ns = bp * seq, db * dseq
    n = np_ + ns

    x = jnp.concatenate([x_prompt.reshape(np_, d), x_sample.reshape(ns, d)], axis=0)

    tm_big = _row_tile(n, 1280)
    tm_exp = 512

    cos_p, sin_p = _rope_tables(jnp.arange(seq))
    tabs_p = (cos_p, sin_p) + _decay_tables(RET_CHUNK, RET_CHUNK, h_ret)
    pos_s = past_len + jnp.arange(RET_CHUNK)
    cos_s, sin_s = _rope_tables(pos_s)
    tabs_s = (cos_s, sin_s) + _decay_tables(RET_CHUNK, dseq, h_ret)
    router_pad = jnp.pad(router, ((0, 0), (0, 0), (0, LANES - n_experts)))
    dense_tiles = jnp.zeros((n // tm_big,), jnp.int32)
    dense_valid = jnp.full((1,), n // tm_big, jnp.int32)

    outs = {k: [] for k in ("kp", "vp", "sp", "bp", "ks", "vs", "ss", "bs")}
    for l in range(depth):
        h = _rmsnorm(x, norm_mix, l)
        p = _matmul(h, w_in, l, tm=tm_big, tn=512, tk=d)

        qg = jnp.tile(q_norm[l], h_sb)[None, :]
        kg = jnp.tile(k_norm[l], h_sb)[None, :]
        qb, kf, kb, vb = _sb_prep(p, qg, kg, w_sb)
        o_sb_p = _sb_prompt(qb, kb, vb, sb_bias[l], bp, seq, tq=128)

        q_s = qb[np_:].reshape(db, dseq, h_sb // 2, 2, HEAD_DIM)
        eye2 = jnp.eye(2, dtype=BF16)
        q_pairs = jnp.einsum("btphd,hg->bphtgd", q_s, eye2).reshape(db, h_sb * dseq, LANES)
        bias_rows = jnp.broadcast_to(jnp.repeat(sb_bias[l], dseq)[:, None], (h_sb * dseq, LANES))
        k_s = kf[np_:].reshape(db, dseq, h_sb, HEAD_DIM)
        v_s = p[np_:, 2 * w_sb:3 * w_sb].reshape(db, dseq, h_sb, HEAD_DIM)
        pad = ((0, 0), (0, PAGE_SIZE - dseq), (0, 0), (0, 0))
        o_sb_s = _sb_sample(q_pairs, bias_rows, jnp.pad(k_s, pad), jnp.pad(v_s, pad), cache_k, cache_v, l,
                            page_table, pages_per_step=4).reshape(ns, w_sb)

        col_ret = 3 * w_sb
        gain_r = ret_norm[l].reshape(1, w_ret)
        s0_p = jnp.zeros((bp, h_ret // 2, LANES, LANES), F32)
        o_r_p, s_p = _retention(p, 0, bp, seq, RET_CHUNK, col_ret, w_ret, tabs_p, s0_p, gain_r)
        p_s = jnp.pad(p[np_:, col_ret:col_ret + 4 * w_ret].reshape(db, dseq, 4 * w_ret),
                      ((0, 0), (0, RET_CHUNK - dseq), (0, 0))).reshape(db * RET_CHUNK, 4 * w_ret)
        o_r_s, s_s = _retention(p_s, 0, db, RET_CHUNK, RET_CHUNK, 0, w_ret, tabs_s,
                                _state_to_bd(state_ret[l]), gain_r)
        o_r_s = o_r_s.reshape(db, RET_CHUNK, w_ret)[:, :dseq].reshape(ns, w_ret)

        col_pool = col_ret + 4 * w_ret
        buf_p = jnp.zeros((bp, POOL_PAD, w_pl), F32)
        o_p_p, nb_p = _pool(p, 0, bp, seq, col_pool, w_pl, buf_p, w_pool, pool_scale, l, 0, BF16)
        buf_s = jnp.pad(state_pool[l], ((0, 0), (1, 0), (0, 0)))
        o_p_s, nb_s = _pool(p, np_, db, dseq, col_pool, w_pl, buf_s, w_pool, pool_scale, l, past_len, F32)

        mix = jnp.concatenate([jnp.concatenate([o_sb_p, o_r_p, o_p_p], axis=1),
                               jnp.concatenate([o_sb_s.astype(BF16), o_r_s, o_p_s.astype(BF16)], axis=1)], axis=0)
        x = _matmul(mix, w_out, l, res=x, tm=tm_big, tn=512, tk=d)

        i = l // 2
        if l % 2 == 0:
            h = _rmsnorm(x, norm_ffn, l)
            act = _gateup(h, w_gate[:, None], w_up[:, None], i, dense_tiles, dense_valid, tm=tm_big, tf=512)
            x = _matmul(act, w_down, i, res=x, tm=tm_big, tn=512, tk=512)
        else:
            slab = _router(x, norm_ffn, l, router_pad[i:i + 1], n_experts)
            src_row, gate_sorted, tile_expert, n_valid, pos1, pos2 = _routing(slab, n_experts, tm_exp)
            hs = _gather_norm(x, norm_ffn, l, src_row, n_valid * (tm_exp // 256), tg=256)
            act = _gateup(hs, w_gate_exp, w_up_exp, i, tile_expert, n_valid, tm=tm_exp, tf=512)
            dsorted = _down(act, w_down_exp, i, tile_expert, n_valid, gate_sorted, tm=tm_exp, tk=512)
            x = _combine(x, dsorted, pos1, pos2, tc=_row_tile(n, 128, 8))

        outs["kp"].append(kf[:np_].reshape(bp, seq, h_sb, HEAD_DIM))
        outs["vp"].append(p[:np_, 2 * w_sb:3 * w_sb].reshape(bp, seq, h_sb, HEAD_DIM))
        outs["sp"].append(_state_from_bd(s_p))
        outs["bp"].append(nb_p[:, 1:])
        outs["ks"].append(k_s)
        outs["vs"].append(v_s)
        outs["ss"].append(_state_from_bd(s_s))
        outs["bs"].append(nb_s[:, 1:])

    return (x[:np_].reshape(bp, seq, d), x[np_:].reshape(db, dseq, d),
            jnp.stack(outs["kp"]), jnp.stack(outs["vp"]), jnp.stack(outs["sp"]), jnp.stack(outs["bp"]),
            jnp.stack(outs["ks"]), jnp.stack(outs["vs"]), jnp.stack(outs["ss"]), jnp.stack(outs["bs"]))
```

```python
import functools

import jax
import jax.numpy as jnp
from jax import lax
from jax.experimental import pallas as pl
from jax.experimental.pallas import tpu as pltpu

F32 = jnp.float32
BF16 = jnp.bfloat16

HEAD_DIM = 64
LANES = 128
PAGE_SIZE = 128
POOL_WINDOWS = (2, 4, 8, 16)
POOL_BUF = max(POOL_WINDOWS) - 1
POOL_PAD = POOL_BUF + 1
RET_CHUNK = 128
ROPE_THETA = 10000.0
TOP_K = 2
EPS = 1e-6
VMEM_LIMIT = 52 * 1024 * 1024
DOWN_VMEM_LIMIT = 58 * 1024 * 1024


def _params(*sem):
    return pltpu.CompilerParams(dimension_semantics=sem, vmem_limit_bytes=VMEM_LIMIT)


def _row_tile(n, target, mult=16):
    best = None
    for t in range(mult, min(n, target) + 1, mult):
        if n % t == 0:
            best = t
    return best if best is not None else n


def _split_bf16(x):
    hi = x.astype(BF16)
    lo = (x - hi.astype(F32)).astype(BF16)
    return hi, lo


def _group_ones():
    r = lax.broadcasted_iota(jnp.int32, (LANES, LANES), 0) // HEAD_DIM
    c = lax.broadcasted_iota(jnp.int32, (LANES, LANES), 1) // HEAD_DIM
    return (r == c).astype(BF16)


def _head_sumsq(x, ones_bd):
    hi, lo = _split_bf16(x * x)
    return (jnp.dot(hi, ones_bd, preferred_element_type=F32)
            + jnp.dot(lo, ones_bd, preferred_element_type=F32))


def _dot_nt(a, b):
    return lax.dot_general(a, b, (((1,), (1,)), ((), ())), preferred_element_type=F32)


def _rms_kernel(x_ref, g_ref, o_ref):
    x = x_ref[...]
    ms = jnp.mean(x * x, axis=-1, keepdims=True)
    o_ref[...] = (x * lax.rsqrt(ms + EPS) * g_ref[0]).astype(o_ref.dtype)


def _rmsnorm(x, g_all, layer):
    n, d = x.shape
    tm = _row_tile(n, 640)
    return pl.pallas_call(
        _rms_kernel,
        grid=(n // tm,),
        in_specs=[pl.BlockSpec((tm, d), lambda i: (i, 0)),
                  pl.BlockSpec((1, 1, d), lambda i: (layer, 0, 0))],
        out_specs=pl.BlockSpec((tm, d), lambda i: (i, 0)),
        out_shape=jax.ShapeDtypeStruct((n, d), BF16),
        compiler_params=_params("parallel"),
        name="rmsnorm",
    )(x, g_all.reshape(g_all.shape[0], 1, d))


def _mm_kernel(a_ref, w_ref, *rest, nk, has_res):
    if has_res:
        r_ref, o_ref = rest
    else:
        (o_ref,) = rest
    prod = jnp.dot(a_ref[...], w_ref[...].astype(BF16), preferred_element_type=F32)
    if nk == 1:
        o_ref[...] = prod + r_ref[...] if has_res else prod
        return
    k = pl.program_id(2)

    @pl.when(k == 0)
    def _():
        o_ref[...] = prod + r_ref[...] if has_res else prod

    @pl.when(k > 0)
    def _():
        o_ref[...] += prod


def _matmul(a, w_all, layer, res=None, *, tm, tn, tk):
    m, kd = a.shape
    nd = w_all.shape[-1]
    nk = kd // tk
    in_specs = [pl.BlockSpec((tm, tk), lambda i, j, k: (i, k)),
                pl.BlockSpec((None, tk, tn), lambda i, j, k: (layer, k, j))]
    args = [a, w_all]
    if res is not None:
        in_specs.append(pl.BlockSpec((tm, tn), lambda i, j, k: (i, j)))
        args.append(res)
    return pl.pallas_call(
        functools.partial(_mm_kernel, nk=nk, has_res=res is not None),
        grid=(m // tm, nd // tn, nk),
        in_specs=in_specs,
        out_specs=pl.BlockSpec((tm, tn), lambda i, j, k: (i, j)),
        out_shape=jax.ShapeDtypeStruct((m, nd), F32),
        compiler_params=_params("parallel", "parallel", "arbitrary"),
        name="matmul",
    )(*args)


def _gateup_kernel(te_ref, nv_ref, a_ref, wg_ref, wu_ref, o_ref):
    i = pl.program_id(1)

    @pl.when(i < nv_ref[0])
    def _():
        a = a_ref[...]
        g = jnp.dot(a, wg_ref[...].astype(BF16), preferred_element_type=F32)
        u = jnp.dot(a, wu_ref[...].astype(BF16), preferred_element_type=F32)
        o_ref[...] = (g * jax.nn.sigmoid(g) * u).astype(o_ref.dtype)

    @pl.when(i >= nv_ref[0])
    def _():
        o_ref[...] = jnp.zeros_like(o_ref)


def _gateup(a, wg_all, wu_all, layer, tile_expert, n_valid, *, tm, tf):
    r, kd = a.shape
    fd = wg_all.shape[-1]
    w_spec = pl.BlockSpec((None, None, kd, tf), lambda j, i, te, nv: (layer, te[i], 0, j))
    return pl.pallas_call(
        _gateup_kernel,
        grid_spec=pltpu.PrefetchScalarGridSpec(
            num_scalar_prefetch=2,
            grid=(fd // tf, r // tm),
            in_specs=[pl.BlockSpec((tm, kd), lambda j, i, te, nv: (jnp.minimum(i, nv[0] - 1), 0)),
                      w_spec, w_spec],
            out_specs=pl.BlockSpec((tm, tf), lambda j, i, te, nv: (i, j)),
        ),
        out_shape=jax.ShapeDtypeStruct((r, fd), BF16),
        compiler_params=_params("parallel", "arbitrary"),
        name="gateup",
    )(tile_expert, n_valid, a, wg_all, wu_all)


def _down_kernel(te_ref, nv_ref, a_ref, w_ref, s_ref, o_ref, wb_ref, *, sub):
    i = pl.program_id(1)
    valid = i < nv_ref[0] * sub

    @pl.when(jnp.logical_or(i == 0, te_ref[i // sub] != te_ref[jnp.maximum(i - 1, 0) // sub]))
    def _():
        wb_ref[...] = w_ref[...].astype(BF16)

    @pl.when(valid)
    def _():
        o_ref[...] = jnp.dot(a_ref[...], wb_ref[...], preferred_element_type=F32) * s_ref[...]

    @pl.when(jnp.logical_not(valid))
    def _():
        o_ref[...] = jnp.zeros_like(o_ref)


def _down(a, wd_all, layer, tile_expert, n_valid, scale, *, tm, tn, sub):
    r, fd = a.shape
    dd = wd_all.shape[-1]

    def row(i, nv):
        return jnp.minimum(i, nv[0] * sub - 1)

    return pl.pallas_call(
        functools.partial(_down_kernel, sub=sub),
        grid_spec=pltpu.PrefetchScalarGridSpec(
            num_scalar_prefetch=2,
            grid=(dd // tn, r // tm),
            in_specs=[pl.BlockSpec((tm, fd), lambda j, i, te, nv: (row(i, nv), 0)),
                      pl.BlockSpec((None, None, fd, tn), lambda j, i, te, nv: (layer, te[i // sub], 0, j)),
                      pl.BlockSpec((tm, 1), lambda j, i, te, nv: (i, 0))],
            out_specs=pl.BlockSpec((tm, tn), lambda j, i, te, nv: (i, j)),
            scratch_shapes=[pltpu.VMEM((fd, tn), BF16)],
        ),
        out_shape=jax.ShapeDtypeStruct((r, dd), F32),
        compiler_params=pltpu.CompilerParams(dimension_semantics=("arbitrary", "arbitrary"),
                                             vmem_limit_bytes=DOWN_VMEM_LIMIT),
        name="expert_down",
    )(tile_expert, n_valid, a, wd_all, scale)


def _router_kernel(x_ref, g_ref, r_ref, o_ref, *, n_experts):
    x = x_ref[...]
    ms = jnp.mean(x * x, axis=-1, keepdims=True)
    h = x * lax.rsqrt(ms + EPS) * g_ref[0]
    h_hi, h_lo = _split_bf16(h)
    r_hi, r_lo = _split_bf16(r_ref[0])
    logits = (jnp.dot(h_hi, r_hi, preferred_element_type=F32)
              + jnp.dot(h_hi, r_lo, preferred_element_type=F32)
              + jnp.dot(h_lo, r_hi, preferred_element_type=F32))
    lane = lax.broadcasted_iota(jnp.int32, logits.shape, 1)
    neg = jnp.float32(-jnp.inf)
    big = jnp.int32(LANES)
    logits = jnp.where(lane < n_experts, logits, neg)
    m1 = jnp.max(logits, axis=-1, keepdims=True)
    i1 = jnp.min(jnp.where(logits == m1, lane, big), axis=-1, keepdims=True)
    rest = jnp.where(lane == i1, neg, logits)
    m2 = jnp.max(rest, axis=-1, keepdims=True)
    i2 = jnp.min(jnp.where(rest == m2, lane, big), axis=-1, keepdims=True)
    e2 = jnp.exp(m2 - m1)
    den = 1.0 + e2
    g1 = 1.0 / den
    g2 = e2 / den
    out = jnp.where(lane == 0, i1.astype(F32),
                    jnp.where(lane == 1, i2.astype(F32),
                              jnp.where(lane == 2, g1, jnp.where(lane == 3, g2, 0.0))))
    o_ref[...] = out


def _router(x, g_all, layer, router_pad, n_experts):
    n, d = x.shape
    tm = _row_tile(n, 640)
    return pl.pallas_call(
        functools.partial(_router_kernel, n_experts=n_experts),
        grid=(n // tm,),
        in_specs=[pl.BlockSpec((tm, d), lambda i: (i, 0)),
                  pl.BlockSpec((1, 1, d), lambda i: (layer, 0, 0)),
                  pl.BlockSpec((1, d, LANES), lambda i: (0, 0, 0))],
        out_specs=pl.BlockSpec((tm, LANES), lambda i: (i, 0)),
        out_shape=jax.ShapeDtypeStruct((n, LANES), F32),
        compiler_params=_params("parallel"),
        name="router",
    )(x, g_all.reshape(g_all.shape[0], 1, d), router_pad)


def _gather_norm_kernel(nv_ref, idx_ref, x_hbm, g_ref, o_ref, buf, sem, *, tg):
    i = pl.program_id(0)

    @pl.when(i < nv_ref[0])
    def _():
        def issue(r, c):
            pltpu.make_async_copy(x_hbm.at[pl.ds(idx_ref[0, 0, r], 1)], buf.at[pl.ds(r, 1)], sem).start()
            return c

        lax.fori_loop(0, tg, issue, 0)
        pltpu.make_async_copy(x_hbm.at[pl.ds(0, tg)], buf, sem).wait()
        x = buf[...]
        ms = jnp.mean(x * x, axis=-1, keepdims=True)
        o_ref[...] = (x * lax.rsqrt(ms + EPS) * g_ref[0]).astype(o_ref.dtype)

    @pl.when(i >= nv_ref[0])
    def _():
        o_ref[...] = jnp.zeros_like(o_ref)


def _gather_norm(x, g_all, layer, src_row, n_valid_tiles, *, tg):
    n, d = x.shape
    r = src_row.shape[0]
    nt = r // tg
    return pl.pallas_call(
        functools.partial(_gather_norm_kernel, tg=tg),
        grid_spec=pltpu.PrefetchScalarGridSpec(
            num_scalar_prefetch=1,
            grid=(nt,),
            in_specs=[pl.BlockSpec((1, 1, tg), lambda i, nv: (i, 0, 0), memory_space=pltpu.SMEM),
                      pl.BlockSpec(memory_space=pl.ANY),
                      pl.BlockSpec((1, 1, d), lambda i, nv: (layer, 0, 0))],
            out_specs=pl.BlockSpec((tg, d), lambda i, nv: (i, 0)),
            scratch_shapes=[pltpu.VMEM((tg, d), F32), pltpu.SemaphoreType.DMA(())],
        ),
        out_shape=jax.ShapeDtypeStruct((r, d), BF16),
        compiler_params=_params("arbitrary"),
        name="gather_norm",
    )(n_valid_tiles, src_row.reshape(nt, 1, tg), x, g_all.reshape(g_all.shape[0], 1, d))


def _combine_kernel(p1_ref, p2_ref, x_ref, d_hbm, o_ref, buf1, buf2, sem, *, tc):
    def issue(r, c):
        pltpu.make_async_copy(d_hbm.at[pl.ds(p1_ref[0, 0, r], 1)], buf1.at[pl.ds(r, 1)], sem.at[0]).start()
        pltpu.make_async_copy(d_hbm.at[pl.ds(p2_ref[0, 0, r], 1)], buf2.at[pl.ds(r, 1)], sem.at[1]).start()
        return c

    lax.fori_loop(0, tc, issue, 0)
    pltpu.make_async_copy(d_hbm.at[pl.ds(0, tc)], buf1, sem.at[0]).wait()
    pltpu.make_async_copy(d_hbm.at[pl.ds(0, tc)], buf2, sem.at[1]).wait()
    o_ref[...] = x_ref[...] + (buf1[...] + buf2[...])


def _combine(x, dsorted, pos1, pos2, *, tc):
    n, d = x.shape
    nt = n // tc
    idx_spec = pl.BlockSpec((1, 1, tc), lambda i: (i, 0, 0), memory_space=pltpu.SMEM)
    return pl.pallas_call(
        functools.partial(_combine_kernel, tc=tc),
        grid=(nt,),
        in_specs=[idx_spec, idx_spec,
                  pl.BlockSpec((tc, d), lambda i: (i, 0)),
                  pl.BlockSpec(memory_space=pl.ANY)],
        out_specs=pl.BlockSpec((tc, d), lambda i: (i, 0)),
        out_shape=jax.ShapeDtypeStruct((n, d), F32),
        scratch_shapes=[pltpu.VMEM((tc, d), F32), pltpu.VMEM((tc, d), F32),
                        pltpu.SemaphoreType.DMA((2,))],
        compiler_params=_params("arbitrary"),
        name="combine",
    )(pos1.reshape(nt, 1, tc), pos2.reshape(nt, 1, tc), x, dsorted)


def _sb_prep_kernel(q_ref, k_ref, v_ref, qg_ref, kg_ref, qb_ref, kf_ref, kb_ref, vb_ref, *, scale):
    ones_bd = _group_ones()
    w = q_ref.shape[1]
    for c in range(w // LANES):
        sl = slice(c * LANES, (c + 1) * LANES)
        q = q_ref[:, sl]
        qn = q * lax.rsqrt(_head_sumsq(q, ones_bd) * (1.0 / HEAD_DIM) + EPS) * qg_ref[:, sl]
        qb_ref[:, sl] = (qn * scale).astype(BF16)
        k = k_ref[:, sl]
        kn = k * lax.rsqrt(_head_sumsq(k, ones_bd) * (1.0 / HEAD_DIM) + EPS) * kg_ref[:, sl]
        kf_ref[:, sl] = kn
        kb_ref[:, sl] = kn.astype(BF16)
    vb_ref[...] = v_ref[...].astype(BF16)


def _sb_prep(p, qg, kg, w_sb):
    n = p.shape[0]
    tm = _row_tile(n, 640)
    blk = lambda c: pl.BlockSpec((tm, w_sb), lambda i: (i, c))
    gain = pl.BlockSpec((1, w_sb), lambda i: (0, 0))
    out = pl.BlockSpec((tm, w_sb), lambda i: (i, 0))
    return pl.pallas_call(
        functools.partial(_sb_prep_kernel, scale=HEAD_DIM ** -0.5),
        grid=(n // tm,),
        in_specs=[blk(0), blk(1), blk(2), gain, gain],
        out_specs=[out, out, out, out],
        out_shape=[jax.ShapeDtypeStruct((n, w_sb), BF16), jax.ShapeDtypeStruct((n, w_sb), F32),
                   jax.ShapeDtypeStruct((n, w_sb), BF16), jax.ShapeDtypeStruct((n, w_sb), BF16)],
        compiler_params=_params("parallel"),
        name="sb_prep",
    )(p, p, p, qg, kg)


def _neg_softplus(z):
    return -(jnp.maximum(z, 0.0) + jnp.log(1.0 + jnp.exp(-jnp.abs(z))))


def _suffix_matrix():
    r = lax.broadcasted_iota(jnp.int32, (2 * LANES, 2 * LANES), 0) % LANES
    c = lax.broadcasted_iota(jnp.int32, (2 * LANES, 2 * LANES), 1)
    return jnp.logical_or(r > c, c >= LANES).astype(BF16)


def _sb_block(z, carry, wmat, mask):
    ls = _neg_softplus(z)
    if mask is not None:
        ls = jnp.where(mask, ls, 0.0)
    hi, lo = _split_bf16(ls)
    n_sub = z.shape[1] // LANES
    tails = [None] * n_sub
    for s in range(n_sub - 1, -1, -1):
        sl = slice(s * LANES, (s + 1) * LANES)
        t = jnp.dot(jnp.concatenate([hi[:, sl], lo[:, sl]], axis=1), wmat, preferred_element_type=F32)
        tails[s] = t[:, :LANES] + carry
        carry = carry + t[:, LANES:]
    tail = tails[0] if n_sub == 1 else jnp.concatenate(tails, axis=1)
    a = jnp.exp(z + ls + tail)
    if mask is not None:
        a = jnp.where(mask, a, 0.0)
    return a, carry


def _sb_prompt_kernel(bias_ref, q_ref, k_ref, v_ref, o_ref, acc_ref, car_ref, *, tq, wide):
    hp = pl.program_id(1)
    qb = pl.program_id(2)
    q = q_ref[...]
    lane = lax.broadcasted_iota(jnp.int32, (tq, LANES), 1)
    first = lane < HEAD_DIM
    zero = jnp.zeros_like(q)
    qh = (jnp.where(first, q, zero), jnp.where(first, zero, q))
    bh = (bias_ref[2 * hp], bias_ref[2 * hp + 1])
    wmat = _suffix_matrix()
    acc_ref[...] = jnp.zeros_like(acc_ref)
    car_ref[...] = jnp.zeros_like(car_ref)

    def group(k0, width, mask):
        kblk = k_ref[pl.ds(k0, width), :]
        vblk = v_ref[pl.ds(k0, width), :]
        for h in range(2):
            z = _dot_nt(qh[h], kblk) + bh[h]
            a, car = _sb_block(z, car_ref[h], wmat, mask)
            car_ref[h] = car
            acc_ref[h] += jnp.dot(a.astype(BF16), vblk, preferred_element_type=F32)

    row = lax.broadcasted_iota(jnp.int32, (tq, tq), 0)
    col = lax.broadcasted_iota(jnp.int32, (tq, tq), 1)
    group(pl.multiple_of(qb * tq, tq), tq, col < row)
    n_wide = (qb * tq) // wide

    def full(j, c):
        group(pl.multiple_of(qb * tq - (j + 1) * wide, tq), wide, None)
        return c

    lax.fori_loop(0, n_wide, full, 0)
    if wide > tq:
        for r in range(wide // tq - 1, 0, -1):
            @pl.when(qb * tq - n_wide * wide >= r * tq)
            def _():
                group((r - 1) * tq, tq, None)

    o_ref[...] = jnp.where(first, acc_ref[0], acc_ref[1]).astype(o_ref.dtype)


def _sb_prompt(qb, kb, vb, bias, n_batch, seq, *, tq, wide):
    w = qb.shape[1]
    nq = seq // tq
    return pl.pallas_call(
        functools.partial(_sb_prompt_kernel, tq=tq, wide=wide),
        grid_spec=pltpu.PrefetchScalarGridSpec(
            num_scalar_prefetch=1,
            grid=(n_batch, w // LANES, nq),
            in_specs=[pl.BlockSpec((tq, LANES), lambda b, h, i, bias: (b * nq + i, h)),
                      pl.BlockSpec((seq, LANES), lambda b, h, i, bias: (b, h)),
                      pl.BlockSpec((seq, LANES), lambda b, h, i, bias: (b, h))],
            out_specs=pl.BlockSpec((tq, LANES), lambda b, h, i, bias: (b * nq + i, h)),
            scratch_shapes=[pltpu.VMEM((2, tq, LANES), F32), pltpu.VMEM((2, tq, LANES), F32)],
        ),
        out_shape=jax.ShapeDtypeStruct((n_batch * seq, w), BF16),
        compiler_params=_params("parallel", "parallel", "arbitrary"),
        name="sb_prompt",
    )(bias, qb, kb, vb)


def _sb_sample_kernel(pt_ref, q_ref, bias_ref, kn_ref, vn_ref, *rest, n_heads, t_new, pages_per_step):
    page_refs = rest[:2 * pages_per_step]
    o_ref, acc_ref, car_ref = rest[2 * pages_per_step:]
    s = pl.program_id(1)
    n_pairs = n_heads // 2
    rows = n_heads * t_new
    wmat = _suffix_matrix()
    bias = bias_ref[...]

    rp = 2 * t_new

    def page(kt_ref, vt_ref, mask, acc, car):
        zs = []
        for hp in range(n_pairs):
            kt = kt_ref[hp * LANES:(hp + 1) * LANES, :].astype(BF16)
            zs.append(jnp.dot(q_ref[0, hp * rp:(hp + 1) * rp, :], kt, preferred_element_type=F32))
        z = jnp.concatenate(zs, axis=0) + bias
        a, car = _sb_block(z, car, wmat, mask)
        a = a.astype(BF16)
        outs = []
        for hp in range(n_pairs):
            vt = vt_ref[hp * LANES:(hp + 1) * LANES, :].astype(BF16)
            outs.append(_dot_nt(a[hp * rp:(hp + 1) * rp, :], vt))
        return acc + jnp.concatenate(outs, axis=0), car

    @pl.when(s == 0)
    def _():
        t_idx = lax.broadcasted_iota(jnp.int32, (rows, PAGE_SIZE), 0) % t_new
        j_idx = lax.broadcasted_iota(jnp.int32, (rows, PAGE_SIZE), 1)
        zero = jnp.zeros((rows, LANES), F32)
        acc, car = page(kn_ref.at[0], vn_ref.at[0], j_idx < t_idx, zero, zero)
        acc_ref[...] = acc
        car_ref[...] = car

    acc = acc_ref[...]
    car = car_ref[...]
    for j in range(pages_per_step):
        acc, car = page(page_refs[2 * j], page_refs[2 * j + 1], None, acc, car)
    acc_ref[...] = acc
    car_ref[...] = car

    @pl.when(s == pl.num_programs(1) - 1)
    def _():
        lane = lax.broadcasted_iota(jnp.int32, (t_new, LANES), 1)
        for hp in range(n_pairs):
            base = hp * 2 * t_new
            o_ref[0, :, hp * LANES:(hp + 1) * LANES] = jnp.where(
                lane < HEAD_DIM, acc_ref[base:base + t_new, :],
                acc_ref[base + t_new:base + 2 * t_new, :]).astype(o_ref.dtype)


def _sb_sample(q_pairs, bias_rows, k_new, v_new, cache_k, cache_v, layer, page_table, *, pages_per_step):
    nb, n_pages = page_table.shape
    rows = q_pairs.shape[1]
    width = cache_k.shape[2]
    n_heads = width // HEAD_DIM
    t_new = rows // n_heads
    steps = n_pages // pages_per_step
    pt = page_table.reshape(-1)

    def page_spec(j):
        def imap(b, s, pt_ref):
            return (layer, pt_ref[b * n_pages + n_pages - 1 - (s * pages_per_step + j)], 0, 0)
        return pl.BlockSpec((None, None, width, PAGE_SIZE), imap)

    page_specs = []
    page_args = []
    for j in range(pages_per_step):
        page_specs += [page_spec(j), page_spec(j)]
        page_args += [cache_k, cache_v]
    new_spec = pl.BlockSpec((1, width, PAGE_SIZE), lambda b, s, pt_ref: (b, 0, 0))
    return pl.pallas_call(
        functools.partial(_sb_sample_kernel, n_heads=n_heads, t_new=t_new, pages_per_step=pages_per_step),
        grid_spec=pltpu.PrefetchScalarGridSpec(
            num_scalar_prefetch=1,
            grid=(nb, steps),
            in_specs=[pl.BlockSpec((1, rows, LANES), lambda b, s, pt_ref: (b, 0, 0)),
                      pl.BlockSpec((rows, LANES), lambda b, s, pt_ref: (0, 0)),
                      new_spec, new_spec] + page_specs,
            out_specs=pl.BlockSpec((1, t_new, n_heads * HEAD_DIM), lambda b, s, pt_ref: (b, 0, 0)),
            scratch_shapes=[pltpu.VMEM((rows, LANES), F32), pltpu.VMEM((rows, LANES), F32)],
        ),
        out_shape=jax.ShapeDtypeStruct((nb, t_new, n_heads * HEAD_DIM), F32),
        compiler_params=_params("parallel", "arbitrary"),
        name="sb_sample",
    )(pt, q_pairs, bias_rows, k_new, v_new, *page_args)


def _rope(x, cos, sin_signed, first_half):
    partner = jnp.where(first_half, pltpu.roll(x, LANES - HEAD_DIM // 2, 1), pltpu.roll(x, HEAD_DIM // 2, 1))
    return x * cos + partner * sin_signed


def _ret_kernel(q_ref, k_ref, v_ref, g_ref, cos_ref, sin_ref, dec_ref, qd_ref, kd_ref, cd_ref, s0_ref, rg_ref,
                o_ref, s_ref, st_ref):
    c = pl.program_id(2)
    rows = q_ref.shape[0]
    lane = lax.broadcasted_iota(jnp.int32, (rows, LANES), 1)
    first = lane < HEAD_DIM
    first_half = (lane % HEAD_DIM) < HEAD_DIM // 2

    @pl.when(c == 0)
    def _():
        st_ref[...] = s0_ref[0, 0]

    cos = cos_ref[...]
    sin = sin_ref[...]
    q = _rope(q_ref[...], cos, sin, first_half)
    k = _rope(k_ref[...], cos, sin, first_half) * (HEAD_DIM ** -0.5)
    qb = q.astype(BF16)
    kb = k.astype(BF16)
    vb = v_ref[...].astype(BF16)
    zero = jnp.zeros_like(qb)
    state = st_ref[...]
    o = jnp.dot(qb, state.astype(BF16), preferred_element_type=F32) * qd_ref[0]
    outs = []
    for h in range(2):
        qh = jnp.where(first, qb, zero) if h == 0 else jnp.where(first, zero, qb)
        att = _dot_nt(qh, kb) * dec_ref[0, h]
        outs.append(jnp.dot(att.astype(BF16), vb, preferred_element_type=F32))
    o = o + jnp.where(first, outs[0], outs[1])
    kd_t = jnp.transpose(k * kd_ref[0]).astype(BF16)
    upd = jnp.dot(kd_t, vb, preferred_element_type=F32)
    r_i = lax.broadcasted_iota(jnp.int32, (LANES, LANES), 0) // HEAD_DIM
    c_i = lax.broadcasted_iota(jnp.int32, (LANES, LANES), 1) // HEAD_DIM
    new_state = state * cd_ref[0] + jnp.where(r_i == c_i, upd, 0.0)
    st_ref[...] = new_state
    s_ref[0, 0] = new_state

    y = o * lax.rsqrt(_head_sumsq(o, _group_ones()) * (1.0 / HEAD_DIM) + EPS) * rg_ref[...]
    g = g_ref[...]
    o_ref[...] = (g * jax.nn.sigmoid(g) * y).astype(o_ref.dtype)


def _retention(p, row0, n_batch, seq, chunk, col0, w_ret, tabs, s0_bd, ret_gain):
    cos, sin, dec, qd, kd, cd = tabs
    n_pairs = w_ret // LANES
    nc = seq // chunk
    rb0 = row0 // chunk
    cb0 = col0 // LANES

    def col(g):
        return pl.BlockSpec((chunk, LANES), lambda b, h, c: (rb0 + b * nc + c, cb0 + g * n_pairs + h))

    return pl.pallas_call(
        _ret_kernel,
        grid=(n_batch, n_pairs, nc),
        in_specs=[col(0), col(1), col(2), col(3),
                  pl.BlockSpec((chunk, LANES), lambda b, h, c: (c, 0)),
                  pl.BlockSpec((chunk, LANES), lambda b, h, c: (c, 0)),
                  pl.BlockSpec((1, 2, chunk, chunk), lambda b, h, c: (h, 0, 0, 0)),
                  pl.BlockSpec((1, chunk, LANES), lambda b, h, c: (h, 0, 0)),
                  pl.BlockSpec((1, chunk, LANES), lambda b, h, c: (h, 0, 0)),
                  pl.BlockSpec((1, LANES, LANES), lambda b, h, c: (h, 0, 0)),
                  pl.BlockSpec((1, 1, LANES, LANES), lambda b, h, c: (b, h, 0, 0)),
                  pl.BlockSpec((1, LANES), lambda b, h, c: (0, h))],
        out_specs=[pl.BlockSpec((chunk, LANES), lambda b, h, c: (b * nc + c, h)),
                   pl.BlockSpec((1, 1, LANES, LANES), lambda b, h, c: (b, h, 0, 0))],
        out_shape=[jax.ShapeDtypeStruct((n_batch * seq, w_ret), BF16),
                   jax.ShapeDtypeStruct((n_batch, n_pairs, LANES, LANES), F32)],
        scratch_shapes=[pltpu.VMEM((LANES, LANES), F32)],
        compiler_params=_params("parallel", "parallel", "arbitrary"),
        name="retention",
    )(p, p, p, p, cos, sin, dec, qd, kd, cd, s0_bd, ret_gain)


def _rope_tables(pos):
    half = HEAD_DIM // 2
    inv = ROPE_THETA ** (-jnp.arange(half, dtype=F32) / half)
    ang = pos.astype(F32)[:, None] * inv[None, :]
    cos = jnp.cos(ang)
    sin = jnp.sin(ang)
    cos_t = jnp.tile(cos, (1, 2 * LANES // HEAD_DIM))
    sin_t = jnp.tile(jnp.concatenate([-sin, sin], axis=1), (1, LANES // HEAD_DIM))
    return cos_t, sin_t


def _decay_tables(rows, c_eff, n_heads):
    log_g = jnp.log(1.0 - 2.0 ** (-5.0 - jnp.arange(n_heads, dtype=F32)))
    idx = jnp.arange(rows, dtype=F32)
    diff = idx[:, None] - idx[None, :]
    decay = jnp.where(diff >= 0, jnp.exp(log_g[:, None, None] * jnp.maximum(diff, 0.0)), 0.0)
    q_dec = jnp.exp(log_g[:, None] * (idx + 1.0))
    k_dec = jnp.exp(log_g[:, None] * (c_eff - 1.0 - idx))
    chunk_dec = jnp.exp(log_g * c_eff)
    n_pairs = n_heads // 2

    def lanes(t):
        return jnp.repeat(t.reshape(n_pairs, 2, rows).transpose(0, 2, 1), HEAD_DIM, axis=2)

    dec = decay.reshape(n_pairs, 2, rows, rows)
    cd = jnp.repeat(chunk_dec.reshape(n_pairs, 2), HEAD_DIM, axis=1)
    cd = jnp.broadcast_to(cd[:, :, None], (n_pairs, LANES, LANES))
    return dec, lanes(q_dec), lanes(k_dec), cd


def _state_to_bd(s):
    b, h = s.shape[0], s.shape[1]
    s = s.reshape(b, h // 2, 2, HEAD_DIM, HEAD_DIM)
    z = jnp.zeros_like(s[:, :, 0])
    top = jnp.concatenate([s[:, :, 0], z], axis=-1)
    bot = jnp.concatenate([z, s[:, :, 1]], axis=-1)
    return jnp.concatenate([top, bot], axis=-2)


def _state_from_bd(s):
    b, hp = s.shape[0], s.shape[1]
    a = s[:, :, :HEAD_DIM, :HEAD_DIM]
    d = s[:, :, HEAD_DIM:, HEAD_DIM:]
    return jnp.stack([a, d], axis=2).reshape(b, hp * 2, HEAD_DIM, HEAD_DIM)


def _pool_kernel(u_ref, buf_ref, w_ref, sc_ref, o_ref, nb_ref, ext_ref, raw_ref, *, start, seq):
    u = u_ref[...]
    hist = buf_ref[0]
    raw_ref[0:POOL_PAD, :] = hist
    raw_ref[POOL_PAD:, :] = u
    r = lax.broadcasted_iota(jnp.int32, hist.shape, 0)
    ext_ref[0:POOL_PAD, :] = jnp.where(start - POOL_PAD + r >= 0, hist, 0.0)
    ext_ref[POOL_PAD:, :] = u
    nb_ref[0] = raw_ref[seq:seq + POOL_PAD, :]
    t = lax.broadcasted_iota(jnp.int32, (seq, LANES), 0)
    for g, w in enumerate(POOL_WINDOWS):
        sl = slice(g * LANES, (g + 1) * LANES)
        ssum = ext_ref[POOL_PAD:POOL_PAD + seq, sl]
        for j in range(1, w):
            ssum = ssum + ext_ref[POOL_PAD - j:POOL_PAD - j + seq, sl]
        n = jnp.minimum(start + t + 1, w).astype(F32)
        pooled = ssum / n - u[:, sl]
        y = jnp.dot(pooled.astype(BF16), w_ref[0, g].astype(BF16), preferred_element_type=F32)
        o_ref[:, sl] = (y * sc_ref[0, :, sl]).astype(o_ref.dtype)


def _pool(p, row0, n_batch, seq, col0, w_pool_w, buf16, w_pool_all, scale_all, layer, start, out_dtype):
    rb0 = row0 // seq
    cb0 = col0 // w_pool_w
    return pl.pallas_call(
        functools.partial(_pool_kernel, start=start, seq=seq),
        grid=(n_batch,),
        in_specs=[pl.BlockSpec((seq, w_pool_w), lambda b: (rb0 + b, cb0)),
                  pl.BlockSpec((1, POOL_PAD, w_pool_w), lambda b: (b, 0, 0)),
                  pl.BlockSpec((1, len(POOL_WINDOWS), LANES, LANES), lambda b: (layer, 0, 0, 0)),
                  pl.BlockSpec((1, 1, w_pool_w), lambda b: (layer, 0, 0))],
        out_specs=[pl.BlockSpec((seq, w_pool_w), lambda b: (b, 0)),
                   pl.BlockSpec((1, POOL_PAD, w_pool_w), lambda b: (b, 0, 0))],
        out_shape=[jax.ShapeDtypeStruct((n_batch * seq, w_pool_w), out_dtype),
                   jax.ShapeDtypeStruct((n_batch, POOL_PAD, w_pool_w), F32)],
        scratch_shapes=[pltpu.VMEM((seq + POOL_PAD, w_pool_w), F32), pltpu.VMEM((seq + POOL_PAD, w_pool_w), F32)],
        compiler_params=_params("parallel"),
        name="pool",
    )(p, buf16, w_pool_all, scale_all.reshape(scale_all.shape[0], 1, w_pool_w))


def _routing(slab, n_experts, tm):
    n = slab.shape[0]
    experts = slab[:, :TOP_K].astype(jnp.int32).reshape(-1)
    gates = slab[:, TOP_K:2 * TOP_K].reshape(-1)
    onehot = (experts[:, None] == jnp.arange(n_experts, dtype=jnp.int32)[None, :]).astype(jnp.int32)
    csum = jnp.cumsum(onehot, axis=0)
    rank = jnp.take_along_axis(csum - onehot, experts[:, None], axis=1)[:, 0]
    counts = csum[-1]
    padded = ((counts + tm - 1) // tm) * tm
    ends = jnp.cumsum(padded)
    starts = ends - padded
    pos = starts[experts] + rank
    n_rows = -(-(TOP_K * n + n_experts * (tm - 1)) // tm) * tm
    n_tiles = n_rows // tm
    token = jnp.arange(TOP_K * n, dtype=jnp.int32) // TOP_K
    src_row = jnp.zeros((n_rows,), jnp.int32).at[pos].set(token)
    gate_sorted = jnp.zeros((n_rows,), F32).at[pos].set(gates)
    tile_start = jnp.arange(n_tiles, dtype=jnp.int32) * tm
    tile_expert = jnp.sum((tile_start[:, None] >= ends[None, :]).astype(jnp.int32), axis=1)
    tile_expert = jnp.minimum(tile_expert, n_experts - 1)
    n_valid = (ends[-1] // tm).astype(jnp.int32).reshape(1)
    pos = pos.reshape(n, TOP_K)
    return src_row, gate_sorted.reshape(n_rows, 1), tile_expert, n_valid, pos[:, 0], pos[:, 1]


def kernel(x_prompt, x_sample, cache_k, cache_v, state_ret, state_pool, page_table, norm_mix, w_in, q_norm,
           k_norm, sb_bias, ret_norm, w_pool, pool_scale, w_out, norm_ffn, w_gate, w_up, w_down, router,
           w_gate_exp, w_up_exp, w_down_exp):
    bp, seq, d = x_prompt.shape
    db, dseq, _ = x_sample.shape
    depth = w_in.shape[0]
    n_pages = page_table.shape[1]
    past_len = n_pages * PAGE_SIZE
    h_sb = cache_k.shape[3]
    w_sb = h_sb * HEAD_DIM
    h_ret = state_ret.shape[2]
    w_ret = h_ret * HEAD_DIM
    w_pl = state_pool.shape[3]
    n_experts = router.shape[-1]
    np_, ns = bp * seq, db * dseq
    n = np_ + ns

    x = jnp.concatenate([x_prompt.reshape(np_, d), x_sample.reshape(ns, d)], axis=0)
    cache_kt = cache_k.transpose(0, 1, 3, 4, 2).reshape(depth, -1, w_sb, PAGE_SIZE)
    cache_vt = cache_v.transpose(0, 1, 3, 4, 2).reshape(depth, -1, w_sb, PAGE_SIZE)

    tm_big = _row_tile(n, 1280)
    tm_exp = 512

    cos_p, sin_p = _rope_tables(jnp.arange(seq))
    tabs_p = (cos_p, sin_p) + _decay_tables(RET_CHUNK, RET_CHUNK, h_ret)
    pos_s = past_len + jnp.arange(RET_CHUNK)
    cos_s, sin_s = _rope_tables(pos_s)
    tabs_s = (cos_s, sin_s) + _decay_tables(RET_CHUNK, dseq, h_ret)
    router_pad = jnp.pad(router, ((0, 0), (0, 0), (0, LANES - n_experts)))
    dense_tiles = jnp.zeros((n // tm_big,), jnp.int32)
    dense_valid = jnp.full((1,), n // tm_big, jnp.int32)

    outs = {k: [] for k in ("kp", "vp", "sp", "bp", "ks", "vs", "ss", "bs")}
    for l in range(depth):
        h = _rmsnorm(x, norm_mix, l)
        p = _matmul(h, w_in, l, tm=tm_big, tn=512, tk=d)

        qg = jnp.tile(q_norm[l], h_sb)[None, :]
        kg = jnp.tile(k_norm[l], h_sb)[None, :]
        qb, kf, kb, vb = _sb_prep(p, qg, kg, w_sb)
        tq = min(256, seq)
        o_sb_p = _sb_prompt(qb, kb, vb, sb_bias[l], bp, seq, tq=tq, wide=min(2 * tq, seq))

        q_s = qb[np_:].reshape(db, dseq, h_sb // 2, 2, HEAD_DIM)
        eye2 = jnp.eye(2, dtype=BF16)
        q_pairs = jnp.einsum("btphd,hg->bphtgd", q_s, eye2).reshape(db, h_sb * dseq, LANES)
        bias_rows = jnp.broadcast_to(jnp.repeat(sb_bias[l], dseq)[:, None], (h_sb * dseq, LANES))
        k_s = kf[np_:].reshape(db, dseq, h_sb, HEAD_DIM)
        v_s = p[np_:, 2 * w_sb:3 * w_sb].reshape(db, dseq, h_sb, HEAD_DIM)
        pad = ((0, 0), (0, 0), (0, PAGE_SIZE - dseq))
        kt_new = jnp.pad(kf[np_:].reshape(db, dseq, w_sb).transpose(0, 2, 1), pad)
        vt_new = jnp.pad(p[np_:, 2 * w_sb:3 * w_sb].reshape(db, dseq, w_sb).transpose(0, 2, 1), pad)
        o_sb_s = _sb_sample(q_pairs, bias_rows, kt_new, vt_new, cache_kt, cache_vt, l,
                            page_table, pages_per_step=min(8, n_pages)).reshape(ns, w_sb)

        col_ret = 3 * w_sb
        gain_r = ret_norm[l].reshape(1, w_ret)
        s0_p = jnp.zeros((bp, h_ret // 2, LANES, LANES), F32)
        o_r_p, s_p = _retention(p, 0, bp, seq, RET_CHUNK, col_ret, w_ret, tabs_p, s0_p, gain_r)
        p_s = jnp.pad(p[np_:, col_ret:col_ret + 4 * w_ret].reshape(db, dseq, 4 * w_ret),
                      ((0, 0), (0, RET_CHUNK - dseq), (0, 0))).reshape(db * RET_CHUNK, 4 * w_ret)
        o_r_s, s_s = _retention(p_s, 0, db, RET_CHUNK, RET_CHUNK, 0, w_ret, tabs_s,
                                _state_to_bd(state_ret[l]), gain_r)
        o_r_s = o_r_s.reshape(db, RET_CHUNK, w_ret)[:, :dseq].reshape(ns, w_ret)

        col_pool = col_ret + 4 * w_ret
        buf_p = jnp.zeros((bp, POOL_PAD, w_pl), F32)
        o_p_p, nb_p = _pool(p, 0, bp, seq, col_pool, w_pl, buf_p, w_pool, pool_scale, l, 0, BF16)
        buf_s = jnp.pad(state_pool[l], ((0, 0), (1, 0), (0, 0)))
        o_p_s, nb_s = _pool(p, np_, db, dseq, col_pool, w_pl, buf_s, w_pool, pool_scale, l, past_len, F32)

        mix = jnp.concatenate([jnp.concatenate([o_sb_p, o_r_p, o_p_p], axis=1),
                               jnp.concatenate([o_sb_s.astype(BF16), o_r_s, o_p_s.astype(BF16)], axis=1)], axis=0)
        x = _matmul(mix, w_out, l, res=x, tm=tm_big, tn=512, tk=d)

        i = l // 2
        if l % 2 == 0:
            h = _rmsnorm(x, norm_ffn, l)
            act = _gateup(h, w_gate[:, None], w_up[:, None], i, dense_tiles, dense_valid, tm=tm_big, tf=512)
            x = _matmul(act, w_down, i, res=x, tm=tm_big, tn=512, tk=act.shape[1] // 2)
        else:
            slab = _router(x, norm_ffn, l, router_pad[i:i + 1], n_experts)
            src_row, gate_sorted, tile_expert, n_valid, pos1, pos2 = _routing(slab, n_experts, tm_exp)
            hs = _gather_norm(x, norm_ffn, l, src_row, n_valid * (tm_exp // 256), tg=256)
            act = _gateup(hs, w_gate_exp, w_up_exp, i, tile_expert, n_valid, tm=tm_exp, tf=512)
            dsorted = _down(act, w_down_exp, i, tile_expert, n_valid, gate_sorted, tm=256, tn=512,
                            sub=tm_exp // 256)
            x = _combine(x, dsorted, pos1, pos2, tc=_row_tile(n, 128, 8))

        outs["kp"].append(kf[:np_].reshape(bp, seq, h_sb, HEAD_DIM))
        outs["vp"].append(p[:np_, 2 * w_sb:3 * w_sb].reshape(bp, seq, h_sb, HEAD_DIM))
        outs["sp"].append(_state_from_bd(s_p))
        outs["bp"].append(nb_p[:, 1:])
        outs["ks"].append(k_s)
        outs["vs"].append(v_s)
        outs["ss"].append(_state_from_bd(s_s))
        outs["bs"].append(nb_s[:, 1:])

    return (x[:np_].reshape(bp, seq, d), x[np_:].reshape(db, dseq, d),
            jnp.stack(outs["kp"]), jnp.stack(outs["vp"]), jnp.stack(outs["sp"]), jnp.stack(outs["bp"]),
            jnp.stack(outs["ks"]), jnp.stack(outs["vs"]), jnp.stack(outs["ss"]), jnp.stack(outs["bs"]))
```

```python
import functools

import jax
import jax.numpy as jnp
from jax import lax
from jax.experimental import pallas as pl
from jax.experimental.pallas import tpu as pltpu

F32 = jnp.float32
BF16 = jnp.bfloat16

HEAD_DIM = 64
LANES = 128
PAGE_SIZE = 128
POOL_WINDOWS = (2, 4, 8, 16)
POOL_BUF = max(POOL_WINDOWS) - 1
POOL_PAD = POOL_BUF + 1
RET_CHUNK = 128
ROPE_THETA = 10000.0
TOP_K = 2
EPS = 1e-6
LOG2E = 1.4426950408889634
VMEM_LIMIT = 52 * 1024 * 1024
DOWN_VMEM_LIMIT = 58 * 1024 * 1024


def _params(*sem):
    return pltpu.CompilerParams(dimension_semantics=sem, vmem_limit_bytes=VMEM_LIMIT)


def _row_tile(n, target, mult=16):
    best = None
    for t in range(mult, min(n, target) + 1, mult):
        if n % t == 0:
            best = t
    return best if best is not None else n


def _split_bf16(x):
    hi = x.astype(BF16)
    lo = (x - hi.astype(F32)).astype(BF16)
    return hi, lo


def _group_ones():
    r = lax.broadcasted_iota(jnp.int32, (LANES, LANES), 0) // HEAD_DIM
    c = lax.broadcasted_iota(jnp.int32, (LANES, LANES), 1) // HEAD_DIM
    return (r == c).astype(BF16)


def _head_sumsq(x, ones_bd):
    hi, lo = _split_bf16(x * x)
    return (jnp.dot(hi, ones_bd, preferred_element_type=F32)
            + jnp.dot(lo, ones_bd, preferred_element_type=F32))


def _dot_nt(a, b):
    return lax.dot_general(a, b, (((1,), (1,)), ((), ())), preferred_element_type=F32)


def _rms_kernel(x_ref, g_ref, o_ref):
    x = x_ref[...]
    ms = jnp.mean(x * x, axis=-1, keepdims=True)
    o_ref[...] = (x * lax.rsqrt(ms + EPS) * g_ref[0]).astype(o_ref.dtype)


def _rmsnorm(x, g_all, layer):
    n, d = x.shape
    tm = _row_tile(n, 640)
    return pl.pallas_call(
        _rms_kernel,
        grid=(n // tm,),
        in_specs=[pl.BlockSpec((tm, d), lambda i: (i, 0)),
                  pl.BlockSpec((1, 1, d), lambda i: (layer, 0, 0))],
        out_specs=pl.BlockSpec((tm, d), lambda i: (i, 0)),
        out_shape=jax.ShapeDtypeStruct((n, d), BF16),
        compiler_params=_params("parallel"),
        name="rmsnorm",
    )(x, g_all.reshape(g_all.shape[0], 1, d))


def _mm_kernel(a_ref, w_ref, *rest, nk, has_res):
    if has_res:
        r_ref, o_ref = rest
    else:
        (o_ref,) = rest
    prod = jnp.dot(a_ref[...], w_ref[...].astype(BF16), preferred_element_type=F32)
    if nk == 1:
        o_ref[...] = prod + r_ref[...] if has_res else prod
        return
    k = pl.program_id(2)

    @pl.when(k == 0)
    def _():
        o_ref[...] = prod + r_ref[...] if has_res else prod

    @pl.when(k > 0)
    def _():
        o_ref[...] += prod


def _matmul(a, w_all, layer, res=None, *, tm, tn, tk):
    m, kd = a.shape
    nd = w_all.shape[-1]
    nk = kd // tk
    in_specs = [pl.BlockSpec((tm, tk), lambda i, j, k: (i, k)),
                pl.BlockSpec((None, tk, tn), lambda i, j, k: (layer, k, j))]
    args = [a, w_all]
    if res is not None:
        in_specs.append(pl.BlockSpec((tm, tn), lambda i, j, k: (i, j)))
        args.append(res)
    return pl.pallas_call(
        functools.partial(_mm_kernel, nk=nk, has_res=res is not None),
        grid=(m // tm, nd // tn, nk),
        in_specs=in_specs,
        out_specs=pl.BlockSpec((tm, tn), lambda i, j, k: (i, j)),
        out_shape=jax.ShapeDtypeStruct((m, nd), F32),
        compiler_params=_params("parallel", "parallel", "arbitrary"),
        name="matmul",
    )(*args)


def _gateup_kernel(te_ref, nv_ref, a_ref, wg_ref, wu_ref, o_ref):
    i = pl.program_id(1)

    @pl.when(i < nv_ref[0])
    def _():
        a = a_ref[...]
        g = jnp.dot(a, wg_ref[...].astype(BF16), preferred_element_type=F32)
        u = jnp.dot(a, wu_ref[...].astype(BF16), preferred_element_type=F32)
        o_ref[...] = (g * jax.nn.sigmoid(g) * u).astype(o_ref.dtype)

    @pl.when(i >= nv_ref[0])
    def _():
        o_ref[...] = jnp.zeros_like(o_ref)


def _gateup(a, wg_all, wu_all, layer, tile_expert, n_valid, *, tm, tf):
    r, kd = a.shape
    fd = wg_all.shape[-1]
    w_spec = pl.BlockSpec((None, None, kd, tf), lambda j, i, te, nv: (layer, te[i], 0, j))
    return pl.pallas_call(
        _gateup_kernel,
        grid_spec=pltpu.PrefetchScalarGridSpec(
            num_scalar_prefetch=2,
            grid=(fd // tf, r // tm),
            in_specs=[pl.BlockSpec((tm, kd), lambda j, i, te, nv: (jnp.minimum(i, nv[0] - 1), 0)),
                      w_spec, w_spec],
            out_specs=pl.BlockSpec((tm, tf), lambda j, i, te, nv: (i, j)),
        ),
        out_shape=jax.ShapeDtypeStruct((r, fd), BF16),
        compiler_params=_params("parallel", "arbitrary"),
        name="gateup",
    )(tile_expert, n_valid, a, wg_all, wu_all)


def _down_kernel(te_ref, nv_ref, a_ref, w_ref, o_ref, wb_ref, *, sub):
    i = pl.program_id(1)
    valid = i < nv_ref[0] * sub

    @pl.when(jnp.logical_or(i == 0, te_ref[i // sub] != te_ref[jnp.maximum(i - 1, 0) // sub]))
    def _():
        wb_ref[...] = w_ref[...].astype(BF16)

    @pl.when(valid)
    def _():
        o_ref[...] = jnp.dot(a_ref[...], wb_ref[...], preferred_element_type=F32)

    @pl.when(jnp.logical_not(valid))
    def _():
        o_ref[...] = jnp.zeros_like(o_ref)


def _down(a, wd_all, layer, tile_expert, n_valid, *, tm, tn, sub):
    r, fd = a.shape
    dd = wd_all.shape[-1]

    def row(i, nv):
        return jnp.minimum(i, nv[0] * sub - 1)

    return pl.pallas_call(
        functools.partial(_down_kernel, sub=sub),
        grid_spec=pltpu.PrefetchScalarGridSpec(
            num_scalar_prefetch=2,
            grid=(dd // tn, r // tm),
            in_specs=[pl.BlockSpec((tm, fd), lambda j, i, te, nv: (row(i, nv), 0)),
                      pl.BlockSpec((None, None, fd, tn), lambda j, i, te, nv: (layer, te[i // sub], 0, j))],
            out_specs=pl.BlockSpec((tm, tn), lambda j, i, te, nv: (i, j)),
            scratch_shapes=[pltpu.VMEM((fd, tn), BF16)],
        ),
        out_shape=jax.ShapeDtypeStruct((r, dd), F32),
        compiler_params=pltpu.CompilerParams(dimension_semantics=("arbitrary", "arbitrary"),
                                             vmem_limit_bytes=DOWN_VMEM_LIMIT),
        name="expert_down",
    )(tile_expert, n_valid, a, wd_all)


def _router_kernel(x_ref, g_ref, r_ref, o_ref, *, n_experts):
    x = x_ref[...]
    ms = jnp.mean(x * x, axis=-1, keepdims=True)
    h = x * lax.rsqrt(ms + EPS) * g_ref[0]
    h_hi, h_lo = _split_bf16(h)
    r_hi, r_lo = _split_bf16(r_ref[0])
    logits = (jnp.dot(h_hi, r_hi, preferred_element_type=F32)
              + jnp.dot(h_hi, r_lo, preferred_element_type=F32)
              + jnp.dot(h_lo, r_hi, preferred_element_type=F32))
    lane = lax.broadcasted_iota(jnp.int32, logits.shape, 1)
    neg = jnp.float32(-jnp.inf)
    big = jnp.int32(LANES)
    logits = jnp.where(lane < n_experts, logits, neg)
    m1 = jnp.max(logits, axis=-1, keepdims=True)
    i1 = jnp.min(jnp.where(logits == m1, lane, big), axis=-1, keepdims=True)
    rest = jnp.where(lane == i1, neg, logits)
    m2 = jnp.max(rest, axis=-1, keepdims=True)
    i2 = jnp.min(jnp.where(rest == m2, lane, big), axis=-1, keepdims=True)
    e2 = jnp.exp(m2 - m1)
    den = 1.0 + e2
    g1 = 1.0 / den
    g2 = e2 / den
    out = jnp.where(lane == 0, i1.astype(F32),
                    jnp.where(lane == 1, i2.astype(F32),
                              jnp.where(lane == 2, g1, jnp.where(lane == 3, g2, 0.0))))
    o_ref[...] = out


def _router(x, g_all, layer, router_pad, n_experts):
    n, d = x.shape
    tm = _row_tile(n, 640)
    return pl.pallas_call(
        functools.partial(_router_kernel, n_experts=n_experts),
        grid=(n // tm,),
        in_specs=[pl.BlockSpec((tm, d), lambda i: (i, 0)),
                  pl.BlockSpec((1, 1, d), lambda i: (layer, 0, 0)),
                  pl.BlockSpec((1, d, LANES), lambda i: (0, 0, 0))],
        out_specs=pl.BlockSpec((tm, LANES), lambda i: (i, 0)),
        out_shape=jax.ShapeDtypeStruct((n, LANES), F32),
        compiler_params=_params("parallel"),
        name="router",
    )(x, g_all.reshape(g_all.shape[0], 1, d), router_pad)


def _start_row_gather(idx_ref, src_hbm, dst, sem, n_rows):
    def issue(r, c):
        pltpu.make_async_copy(src_hbm.at[pl.ds(idx_ref[0, 0, r], 1)], dst.at[pl.ds(r, 1)], sem).start()
        return c

    lax.fori_loop(0, n_rows, issue, 0)


def _wait_row_gather(src_hbm, dst, sem, n_rows):
    pltpu.make_async_copy(src_hbm.at[pl.ds(0, n_rows)], dst, sem).wait()


def _gather_norm_kernel(nv_ref, idx_ref, nxt_ref, x_hbm, g_ref, o_ref, buf, sem, *, tg):
    i = pl.program_id(0)
    nv = nv_ref[0]
    slot = i % 2

    @pl.when(jnp.logical_and(i == 0, nv > 0))
    def _():
        _start_row_gather(idx_ref, x_hbm, buf.at[0], sem.at[0], tg)

    @pl.when(i + 1 < nv)
    def _():
        _start_row_gather(nxt_ref, x_hbm, buf.at[1 - slot], sem.at[1 - slot], tg)

    @pl.when(i < nv)
    def _():
        _wait_row_gather(x_hbm, buf.at[slot], sem.at[slot], tg)
        x = buf[slot]
        ms = jnp.mean(x * x, axis=-1, keepdims=True)
        o_ref[...] = (x * lax.rsqrt(ms + EPS) * g_ref[0]).astype(o_ref.dtype)

    @pl.when(i >= nv)
    def _():
        o_ref[...] = jnp.zeros_like(o_ref)


def _gather_norm(x, g_all, layer, src_row, n_valid_tiles, *, tg):
    n, d = x.shape
    r = src_row.shape[0]
    nt = r // tg
    idx = src_row.reshape(nt, 1, tg)
    return pl.pallas_call(
        functools.partial(_gather_norm_kernel, tg=tg),
        grid_spec=pltpu.PrefetchScalarGridSpec(
            num_scalar_prefetch=1,
            grid=(nt,),
            in_specs=[pl.BlockSpec((1, 1, tg), lambda i, nv: (i, 0, 0), memory_space=pltpu.SMEM),
                      pl.BlockSpec((1, 1, tg), lambda i, nv: (jnp.minimum(i + 1, nt - 1), 0, 0),
                                   memory_space=pltpu.SMEM),
                      pl.BlockSpec(memory_space=pl.ANY),
                      pl.BlockSpec((1, 1, d), lambda i, nv: (layer, 0, 0))],
            out_specs=pl.BlockSpec((tg, d), lambda i, nv: (i, 0)),
            scratch_shapes=[pltpu.VMEM((2, tg, d), F32), pltpu.SemaphoreType.DMA((2,))],
        ),
        out_shape=jax.ShapeDtypeStruct((r, d), BF16),
        compiler_params=_params("arbitrary"),
        name="gather_norm",
    )(n_valid_tiles, idx, idx, x, g_all.reshape(g_all.shape[0], 1, d))


def _combine_kernel(idx_ref, nxt_ref, x_ref, g_ref, d_hbm, o_ref, buf, sem, *, tc):
    i = pl.program_id(0)
    slot = i % 2

    @pl.when(i == 0)
    def _():
        _start_row_gather(idx_ref, d_hbm, buf.at[0], sem.at[0], 2 * tc)

    @pl.when(i + 1 < pl.num_programs(0))
    def _():
        _start_row_gather(nxt_ref, d_hbm, buf.at[1 - slot], sem.at[1 - slot], 2 * tc)

    _wait_row_gather(d_hbm, buf.at[slot], sem.at[slot], 2 * tc)
    g = g_ref[...]
    g1 = g[:, TOP_K:TOP_K + 1]
    g2 = g[:, TOP_K + 1:TOP_K + 2]
    o_ref[...] = x_ref[...] + (g1 * buf[slot, 0:tc, :] + g2 * buf[slot, tc:2 * tc, :])


def _combine(x, dsorted, pos1, pos2, slab, *, tc):
    n, d = x.shape
    nt = n // tc
    idx = jnp.concatenate([pos1.reshape(nt, 1, tc), pos2.reshape(nt, 1, tc)], axis=2)
    return pl.pallas_call(
        functools.partial(_combine_kernel, tc=tc),
        grid=(nt,),
        in_specs=[pl.BlockSpec((1, 1, 2 * tc), lambda i: (i, 0, 0), memory_space=pltpu.SMEM),
                  pl.BlockSpec((1, 1, 2 * tc), lambda i: (jnp.minimum(i + 1, nt - 1), 0, 0),
                               memory_space=pltpu.SMEM),
                  pl.BlockSpec((tc, d), lambda i: (i, 0)),
                  pl.BlockSpec((tc, LANES), lambda i: (i, 0)),
                  pl.BlockSpec(memory_space=pl.ANY)],
        out_specs=pl.BlockSpec((tc, d), lambda i: (i, 0)),
        out_shape=jax.ShapeDtypeStruct((n, d), F32),
        scratch_shapes=[pltpu.VMEM((2, 2 * tc, d), F32), pltpu.SemaphoreType.DMA((2,))],
        compiler_params=_params("arbitrary"),
        name="combine",
    )(idx, idx, x, slab, dsorted)


def _sb_prep_kernel(q_ref, k_ref, v_ref, qg_ref, kg_ref, qb_ref, kf_ref, kb_ref, vb_ref, *, scale):
    ones_bd = _group_ones()
    w = q_ref.shape[1]
    for c in range(w // LANES):
        sl = slice(c * LANES, (c + 1) * LANES)
        q = q_ref[:, sl]
        qn = q * lax.rsqrt(_head_sumsq(q, ones_bd) * (1.0 / HEAD_DIM) + EPS) * qg_ref[:, sl]
        qb_ref[:, sl] = (qn * scale).astype(BF16)
        k = k_ref[:, sl]
        kn = k * lax.rsqrt(_head_sumsq(k, ones_bd) * (1.0 / HEAD_DIM) + EPS) * kg_ref[:, sl]
        kf_ref[:, sl] = kn
        kb_ref[:, sl] = kn.astype(BF16)
    vb_ref[...] = v_ref[...].astype(BF16)


def _sb_prep(p, qg, kg, w_sb):
    n = p.shape[0]
    tm = _row_tile(n, 640)
    blk = lambda c: pl.BlockSpec((tm, w_sb), lambda i: (i, c))
    gain = pl.BlockSpec((1, w_sb), lambda i: (0, 0))
    out = pl.BlockSpec((tm, w_sb), lambda i: (i, 0))
    return pl.pallas_call(
        functools.partial(_sb_prep_kernel, scale=HEAD_DIM ** -0.5),
        grid=(n // tm,),
        in_specs=[blk(0), blk(1), blk(2), gain, gain],
        out_specs=[out, out, out, out],
        out_shape=[jax.ShapeDtypeStruct((n, w_sb), BF16), jax.ShapeDtypeStruct((n, w_sb), F32),
                   jax.ShapeDtypeStruct((n, w_sb), BF16), jax.ShapeDtypeStruct((n, w_sb), BF16)],
        compiler_params=_params("parallel"),
        name="sb_prep",
    )(p, p, p, qg, kg)


def _softplus(z):
    return jnp.maximum(z, 0.0) + jnp.log(1.0 + jnp.exp2(jnp.abs(z) * (-LOG2E)))


def _suffix_matrix():
    r = lax.broadcasted_iota(jnp.int32, (2 * LANES, 2 * LANES), 0) % LANES
    c = lax.broadcasted_iota(jnp.int32, (2 * LANES, 2 * LANES), 1)
    return jnp.logical_or(r > c, c >= LANES).astype(BF16)


def _sb_block(z, carry, wmat, mask):
    sp, ts = _sb_suffix_sums(z, wmat, mask)
    return _sb_weights(z, sp, ts, carry, mask)


def _sb_suffix_sums(z, wmat, mask):
    sp = _softplus(z)
    if mask is not None:
        sp = jnp.where(mask, sp, 0.0)
    hi, lo = _split_bf16(sp)
    ts = []
    for s in range(z.shape[1] // LANES):
        sl = slice(s * LANES, (s + 1) * LANES)
        ts.append(jnp.dot(jnp.concatenate([hi[:, sl], lo[:, sl]], axis=1), wmat, preferred_element_type=F32))
    return sp, ts


def _sb_weights(z, sp, ts, carry, mask):
    n_sub = len(ts)
    tails = [None] * n_sub
    for s in range(n_sub - 1, -1, -1):
        tails[s] = ts[s][:, :LANES] + carry
        carry = carry + ts[s][:, LANES:]
    tail = tails[0] if n_sub == 1 else jnp.concatenate(tails, axis=1)
    a = jnp.exp(z - sp - tail)
    if mask is not None:
        a = jnp.where(mask, a, 0.0)
    return a, carry


def _sb_prompt_kernel(bias_ref, q_ref, k_ref, v_ref, o_ref, acc_ref, car_ref, *, tq, wide):
    hp = pl.program_id(1)
    qb = pl.program_id(2)
    q = q_ref[...]
    lane = lax.broadcasted_iota(jnp.int32, (tq, LANES), 1)
    first = lane < HEAD_DIM
    zero = jnp.zeros_like(q)
    qh = (jnp.where(first, q, zero), jnp.where(first, zero, q))
    bh = (bias_ref[2 * hp], bias_ref[2 * hp + 1])
    wmat = _suffix_matrix()
    acc_ref[...] = jnp.zeros_like(acc_ref)
    car_ref[...] = jnp.zeros_like(car_ref)

    def group(k0, width, mask):
        kblk = k_ref[pl.ds(k0, width), :]
        vblk = v_ref[pl.ds(k0, width), :]
        for h in range(2):
            z = _dot_nt(qh[h], kblk) + bh[h]
            a, car = _sb_block(z, car_ref[h], wmat, mask)
            car_ref[h] = car
            acc_ref[h] += jnp.dot(a.astype(BF16), vblk, preferred_element_type=F32)

    row = lax.broadcasted_iota(jnp.int32, (tq, tq), 0)
    col = lax.broadcasted_iota(jnp.int32, (tq, tq), 1)
    group(pl.multiple_of(qb * tq, tq), tq, col < row)
    n_wide = (qb * tq) // wide

    def full(j, c):
        group(pl.multiple_of(qb * tq - (j + 1) * wide, tq), wide, None)
        return c

    lax.fori_loop(0, n_wide, full, 0)
    if wide > tq:
        for r in range(wide // tq - 1, 0, -1):
            @pl.when(qb * tq - n_wide * wide >= r * tq)
            def _():
                group((r - 1) * tq, tq, None)

    o_ref[...] = jnp.where(first, acc_ref[0], acc_ref[1]).astype(o_ref.dtype)


def _sb_prompt(qb, kb, vb, bias, n_batch, seq, into, *, tq, wide):
    w = qb.shape[1]
    nq = seq // tq
    return pl.pallas_call(
        _drop_ref(functools.partial(_sb_prompt_kernel, tq=tq, wide=wide), 4),
        grid_spec=pltpu.PrefetchScalarGridSpec(
            num_scalar_prefetch=1,
            grid=(n_batch, w // LANES, nq),
            in_specs=[pl.BlockSpec((tq, LANES), lambda b, h, i, bias: (b * nq + i, h)),
                      pl.BlockSpec((seq, LANES), lambda b, h, i, bias: (b, h)),
                      pl.BlockSpec((seq, LANES), lambda b, h, i, bias: (b, h)),
                      pl.BlockSpec(memory_space=pl.ANY)],
            out_specs=pl.BlockSpec((tq, LANES), lambda b, h, i, bias: (b * nq + i, h)),
            scratch_shapes=[pltpu.VMEM((2, tq, LANES), F32), pltpu.VMEM((2, tq, LANES), F32)],
        ),
        out_shape=jax.ShapeDtypeStruct(into.shape, into.dtype),
        input_output_aliases={4: 0},
        compiler_params=_params("parallel", "parallel", "arbitrary"),
        name="sb_prompt",
    )(bias, qb, kb, vb, into)


def _sb_sample_kernel(pt_ref, q_ref, bias_ref, kn_ref, vn_ref, *rest, n_heads, t_new, pages_per_step):
    page_refs = rest[:2 * pages_per_step]
    o_ref, acc_ref, car_ref = rest[2 * pages_per_step:]
    s = pl.program_id(1)
    n_pairs = n_heads // 2
    rows = n_heads * t_new
    wmat = _suffix_matrix()
    bias = bias_ref[...]

    rp = 2 * t_new

    def logits(kt_ref):
        zs = []
        for hp in range(n_pairs):
            kt = kt_ref[hp * LANES:(hp + 1) * LANES, :].astype(BF16)
            zs.append(jnp.dot(q_ref[0, hp * rp:(hp + 1) * rp, :], kt, preferred_element_type=F32))
        return jnp.concatenate(zs, axis=0) + bias

    def values(a, vt_ref):
        a = a.astype(BF16)
        outs = []
        for hp in range(n_pairs):
            vt = vt_ref[hp * LANES:(hp + 1) * LANES, :].astype(BF16)
            outs.append(_dot_nt(a[hp * rp:(hp + 1) * rp, :], vt))
        return jnp.concatenate(outs, axis=0)

    def pages(k_refs, v_refs, mask, acc, car):
        zs = [logits(r) for r in k_refs]
        sums = [_sb_suffix_sums(z, wmat, mask) for z in zs]
        weights = []
        for z, (ls, ts) in zip(zs, sums):
            a, car = _sb_weights(z, ls, ts, car, mask)
            weights.append(a)
        for a, r in zip(weights, v_refs):
            acc = acc + values(a, r)
        return acc, car

    @pl.when(s == 0)
    def _():
        t_idx = lax.broadcasted_iota(jnp.int32, (rows, PAGE_SIZE), 0) % t_new
        j_idx = lax.broadcasted_iota(jnp.int32, (rows, PAGE_SIZE), 1)
        zero = jnp.zeros((rows, LANES), F32)
        acc, car = pages([kn_ref.at[0]], [vn_ref.at[0]], j_idx < t_idx, zero, zero)
        acc_ref[...] = acc
        car_ref[...] = car

    acc, car = pages(page_refs[0::2], page_refs[1::2], None, acc_ref[...], car_ref[...])
    acc_ref[...] = acc
    car_ref[...] = car

    @pl.when(s == pl.num_programs(1) - 1)
    def _():
        lane = lax.broadcasted_iota(jnp.int32, (t_new, LANES), 1)
        for hp in range(n_pairs):
            base = hp * 2 * t_new
            o_ref[0, :, hp * LANES:(hp + 1) * LANES] = jnp.where(
                lane < HEAD_DIM, acc_ref[base:base + t_new, :],
                acc_ref[base + t_new:base + 2 * t_new, :]).astype(o_ref.dtype)


def _sb_sample(q_pairs, bias_rows, k_new, v_new, cache_k, cache_v, layer, page_table, *, pages_per_step):
    nb, n_pages = page_table.shape
    rows = q_pairs.shape[1]
    width = cache_k.shape[2]
    n_heads = width // HEAD_DIM
    t_new = rows // n_heads
    steps = n_pages // pages_per_step
    pt = page_table.reshape(-1)

    def page_spec(j):
        def imap(b, s, pt_ref):
            return (layer, pt_ref[b * n_pages + n_pages - 1 - (s * pages_per_step + j)], 0, 0)
        return pl.BlockSpec((None, None, width, PAGE_SIZE), imap)

    page_specs = []
    page_args = []
    for j in range(pages_per_step):
        page_specs += [page_spec(j), page_spec(j)]
        page_args += [cache_k, cache_v]
    new_spec = pl.BlockSpec((1, width, PAGE_SIZE), lambda b, s, pt_ref: (b, 0, 0))
    return pl.pallas_call(
        functools.partial(_sb_sample_kernel, n_heads=n_heads, t_new=t_new, pages_per_step=pages_per_step),
        grid_spec=pltpu.PrefetchScalarGridSpec(
            num_scalar_prefetch=1,
            grid=(nb, steps),
            in_specs=[pl.BlockSpec((1, rows, LANES), lambda b, s, pt_ref: (b, 0, 0)),
                      pl.BlockSpec((rows, LANES), lambda b, s, pt_ref: (0, 0)),
                      new_spec, new_spec] + page_specs,
            out_specs=pl.BlockSpec((1, t_new, n_heads * HEAD_DIM), lambda b, s, pt_ref: (b, 0, 0)),
            scratch_shapes=[pltpu.VMEM((rows, LANES), F32), pltpu.VMEM((rows, LANES), F32)],
        ),
        out_shape=jax.ShapeDtypeStruct((nb, t_new, n_heads * HEAD_DIM), F32),
        compiler_params=_params("parallel", "arbitrary"),
        name="sb_sample",
    )(pt, q_pairs, bias_rows, k_new, v_new, *page_args)


def _rope(x, cos, sin_signed, first_half):
    partner = jnp.where(first_half, pltpu.roll(x, LANES - HEAD_DIM // 2, 1), pltpu.roll(x, HEAD_DIM // 2, 1))
    return x * cos + partner * sin_signed


def _ret_kernel(q_ref, k_ref, v_ref, g_ref, cos_ref, sin_ref, dec_ref, qd_ref, kd_ref, cd_ref, s0_ref, rg_ref,
                o_ref, s_ref, st_ref):
    c = pl.program_id(1)
    rows = q_ref.shape[0]
    n_pairs = q_ref.shape[1] // LANES
    lane = lax.broadcasted_iota(jnp.int32, (rows, LANES), 1)
    first = lane < HEAD_DIM
    first_half = (lane % HEAD_DIM) < HEAD_DIM // 2
    r_i = lax.broadcasted_iota(jnp.int32, (LANES, LANES), 0) // HEAD_DIM
    c_i = lax.broadcasted_iota(jnp.int32, (LANES, LANES), 1) // HEAD_DIM
    same_head = r_i == c_i
    ones_bd = _group_ones()

    @pl.when(c == 0)
    def _():
        st_ref[...] = s0_ref[0]

    cos = cos_ref[...]
    sin = sin_ref[...]
    pairs = range(n_pairs)
    sl = [slice(h * LANES, (h + 1) * LANES) for h in pairs]
    k = [_rope(k_ref[:, sl[h]], cos, sin, first_half) * (HEAD_DIM ** -0.5) for h in pairs]
    qb = [_rope(q_ref[:, sl[h]], cos, sin, first_half).astype(BF16) for h in pairs]
    kb = [k[h].astype(BF16) for h in pairs]
    vb = [v_ref[:, sl[h]].astype(BF16) for h in pairs]
    zero = jnp.zeros_like(qb[0])
    state = [st_ref[h] for h in pairs]
    cross = [jnp.dot(qb[h], state[h].astype(BF16), preferred_element_type=F32) for h in pairs]
    att = [[_dot_nt(jnp.where(first, qb[h], zero) if hh == 0 else jnp.where(first, zero, qb[h]), kb[h])
            for hh in range(2)] for h in pairs]
    upd = [jnp.dot(jnp.transpose(k[h] * kd_ref[h]).astype(BF16), vb[h], preferred_element_type=F32)
           for h in pairs]
    inner = [[jnp.dot((att[h][hh] * dec_ref[h, hh]).astype(BF16), vb[h], preferred_element_type=F32)
              for hh in range(2)] for h in pairs]
    for h in pairs:
        new_state = state[h] * cd_ref[h] + jnp.where(same_head, upd[h], 0.0)
        st_ref[h] = new_state
        s_ref[0, h] = new_state
    o = [cross[h] * qd_ref[h] + jnp.where(first, inner[h][0], inner[h][1]) for h in pairs]
    ss = [_head_sumsq(o[h], ones_bd) for h in pairs]
    for h in pairs:
        y = o[h] * lax.rsqrt(ss[h] * (1.0 / HEAD_DIM) + EPS) * rg_ref[:, sl[h]]
        g = g_ref[:, sl[h]]
        o_ref[:, sl[h]] = (g * jax.nn.sigmoid(g) * y).astype(o_ref.dtype)


def _drop_ref(body, index):
    def wrapped(*refs):
        return body(*refs[:index], *refs[index + 1:])
    return wrapped


def _retention(p, row0, n_batch, seq, chunk, col0, w_ret, tabs, s0_bd, ret_gain, into=None, into_col=0):
    cos, sin, dec, qd, kd, cd = tabs
    n_pairs = w_ret // LANES
    nc = seq // chunk
    rb0 = row0 // chunk
    cb0 = col0 // w_ret

    def col(g):
        return pl.BlockSpec((chunk, w_ret), lambda b, c: (rb0 + b * nc + c, cb0 + g))

    def whole(a):
        return pl.BlockSpec(a.shape, lambda b, c: (0,) * a.ndim)

    in_specs = [col(0), col(1), col(2), col(3),
                pl.BlockSpec((chunk, LANES), lambda b, c: (c, 0)),
                pl.BlockSpec((chunk, LANES), lambda b, c: (c, 0)),
                whole(dec), whole(qd), whole(kd), whole(cd),
                pl.BlockSpec((1, n_pairs, LANES, LANES), lambda b, c: (b, 0, 0, 0)),
                whole(ret_gain)]
    args = [p, p, p, p, cos, sin, dec, qd, kd, cd, s0_bd, ret_gain]
    body, aliases = _ret_kernel, {}
    o_shape = jax.ShapeDtypeStruct((n_batch * seq, w_ret), BF16)
    if into is not None:
        body, aliases = _drop_ref(_ret_kernel, len(args)), {len(args): 0}
        in_specs.append(pl.BlockSpec(memory_space=pl.ANY))
        args.append(into)
        o_shape = jax.ShapeDtypeStruct(into.shape, into.dtype)
    return pl.pallas_call(
        body,
        grid=(n_batch, nc),
        in_specs=in_specs,
        out_specs=[pl.BlockSpec((chunk, w_ret), lambda b, c: (b * nc + c, into_col)),
                   pl.BlockSpec((1, n_pairs, LANES, LANES), lambda b, c: (b, 0, 0, 0))],
        out_shape=[o_shape, jax.ShapeDtypeStruct((n_batch, n_pairs, LANES, LANES), F32)],
        scratch_shapes=[pltpu.VMEM((n_pairs, LANES, LANES), F32)],
        input_output_aliases=aliases,
        compiler_params=_params("parallel", "arbitrary"),
        name="retention",
    )(*args)


def _rope_tables(pos):
    half = HEAD_DIM // 2
    inv = ROPE_THETA ** (-jnp.arange(half, dtype=F32) / half)
    ang = pos.astype(F32)[:, None] * inv[None, :]
    cos = jnp.cos(ang)
    sin = jnp.sin(ang)
    cos_t = jnp.tile(cos, (1, 2 * LANES // HEAD_DIM))
    sin_t = jnp.tile(jnp.concatenate([-sin, sin], axis=1), (1, LANES // HEAD_DIM))
    return cos_t, sin_t


def _decay_tables(rows, c_eff, n_heads):
    log_g = jnp.log(1.0 - 2.0 ** (-5.0 - jnp.arange(n_heads, dtype=F32)))
    idx = jnp.arange(rows, dtype=F32)
    diff = idx[:, None] - idx[None, :]
    decay = jnp.where(diff >= 0, jnp.exp(log_g[:, None, None] * jnp.maximum(diff, 0.0)), 0.0)
    q_dec = jnp.exp(log_g[:, None] * (idx + 1.0))
    k_dec = jnp.exp(log_g[:, None] * (c_eff - 1.0 - idx))
    chunk_dec = jnp.exp(log_g * c_eff)
    n_pairs = n_heads // 2

    def lanes(t):
        return jnp.repeat(t.reshape(n_pairs, 2, rows).transpose(0, 2, 1), HEAD_DIM, axis=2)

    dec = decay.reshape(n_pairs, 2, rows, rows)
    cd = jnp.repeat(chunk_dec.reshape(n_pairs, 2), HEAD_DIM, axis=1)
    cd = jnp.broadcast_to(cd[:, :, None], (n_pairs, LANES, LANES))
    return dec, lanes(q_dec), lanes(k_dec), cd


def _state_to_bd(s):
    b, h = s.shape[0], s.shape[1]
    s = s.reshape(b, h // 2, 2, HEAD_DIM, HEAD_DIM)
    z = jnp.zeros_like(s[:, :, 0])
    top = jnp.concatenate([s[:, :, 0], z], axis=-1)
    bot = jnp.concatenate([z, s[:, :, 1]], axis=-1)
    return jnp.concatenate([top, bot], axis=-2)


def _state_from_bd(s):
    b, hp = s.shape[0], s.shape[1]
    a = s[:, :, :HEAD_DIM, :HEAD_DIM]
    d = s[:, :, HEAD_DIM:, HEAD_DIM:]
    return jnp.stack([a, d], axis=2).reshape(b, hp * 2, HEAD_DIM, HEAD_DIM)


def _pool_kernel(u_ref, buf_ref, w_ref, sc_ref, o_ref, nb_ref, ext_ref, raw_ref, *, start, seq):
    u = u_ref[...]
    hist = buf_ref[0]
    raw_ref[0:POOL_PAD, :] = hist
    raw_ref[POOL_PAD:, :] = u
    r = lax.broadcasted_iota(jnp.int32, hist.shape, 0)
    ext_ref[0:POOL_PAD, :] = jnp.where(start - POOL_PAD + r >= 0, hist, 0.0)
    ext_ref[POOL_PAD:, :] = u
    nb_ref[0] = raw_ref[seq:seq + POOL_PAD, :]
    t = lax.broadcasted_iota(jnp.int32, (seq, LANES), 0)
    for g, w in enumerate(POOL_WINDOWS):
        sl = slice(g * LANES, (g + 1) * LANES)
        ssum = ext_ref[POOL_PAD:POOL_PAD + seq, sl]
        for j in range(1, w):
            ssum = ssum + ext_ref[POOL_PAD - j:POOL_PAD - j + seq, sl]
        n = jnp.minimum(start + t + 1, w).astype(F32)
        pooled = ssum / n - u[:, sl]
        y = jnp.dot(pooled.astype(BF16), w_ref[0, g].astype(BF16), preferred_element_type=F32)
        o_ref[:, sl] = (y * sc_ref[0, :, sl]).astype(o_ref.dtype)


def _pool(p, row0, n_batch, seq, col0, w_pool_w, buf16, w_pool_all, scale_all, layer, start, out_dtype,
          into=None, into_col=0):
    rb0 = row0 // seq
    cb0 = col0 // w_pool_w
    body = functools.partial(_pool_kernel, start=start, seq=seq)
    in_specs = [pl.BlockSpec((seq, w_pool_w), lambda b: (rb0 + b, cb0)),
                pl.BlockSpec((1, POOL_PAD, w_pool_w), lambda b: (b, 0, 0)),
                pl.BlockSpec((1, len(POOL_WINDOWS), LANES, LANES), lambda b: (layer, 0, 0, 0)),
                pl.BlockSpec((1, 1, w_pool_w), lambda b: (layer, 0, 0))]
    args = [p, buf16, w_pool_all, scale_all.reshape(scale_all.shape[0], 1, w_pool_w)]
    aliases = {}
    o_shape = jax.ShapeDtypeStruct((n_batch * seq, w_pool_w), out_dtype)
    if into is not None:
        body, aliases = _drop_ref(body, len(args)), {len(args): 0}
        in_specs.append(pl.BlockSpec(memory_space=pl.ANY))
        args.append(into)
        o_shape = jax.ShapeDtypeStruct(into.shape, into.dtype)
    return pl.pallas_call(
        body,
        grid=(n_batch,),
        in_specs=in_specs,
        out_specs=[pl.BlockSpec((seq, w_pool_w), lambda b: (b, into_col)),
                   pl.BlockSpec((1, POOL_PAD, w_pool_w), lambda b: (b, 0, 0))],
        out_shape=[o_shape, jax.ShapeDtypeStruct((n_batch, POOL_PAD, w_pool_w), F32)],
        scratch_shapes=[pltpu.VMEM((seq + POOL_PAD, w_pool_w), F32), pltpu.VMEM((seq + POOL_PAD, w_pool_w), F32)],
        input_output_aliases=aliases,
        compiler_params=_params("parallel"),
        name="pool",
    )(*args)


def _routing(slab, n_experts, tm):
    n = slab.shape[0]
    experts = slab[:, :TOP_K].astype(jnp.int32).reshape(-1)
    onehot = (experts[:, None] == jnp.arange(n_experts, dtype=jnp.int32)[None, :]).astype(jnp.int32)
    csum = jnp.cumsum(onehot, axis=0)
    rank = jnp.take_along_axis(csum - onehot, experts[:, None], axis=1)[:, 0]
    counts = csum[-1]
    padded = ((counts + tm - 1) // tm) * tm
    ends = jnp.cumsum(padded)
    starts = ends - padded
    pos = starts[experts] + rank
    n_rows = -(-(TOP_K * n + n_experts * (tm - 1)) // tm) * tm
    n_tiles = n_rows // tm
    token = jnp.arange(TOP_K * n, dtype=jnp.int32) // TOP_K
    src_row = jnp.zeros((n_rows,), jnp.int32).at[pos].set(token)
    tile_start = jnp.arange(n_tiles, dtype=jnp.int32) * tm
    tile_expert = jnp.sum((tile_start[:, None] >= ends[None, :]).astype(jnp.int32), axis=1)
    tile_expert = jnp.minimum(tile_expert, n_experts - 1)
    n_valid = (ends[-1] // tm).astype(jnp.int32).reshape(1)
    pos = pos.reshape(n, TOP_K)
    return src_row, tile_expert, n_valid, pos[:, 0], pos[:, 1]


def kernel(x_prompt, x_sample, cache_k, cache_v, state_ret, state_pool, page_table, norm_mix, w_in, q_norm,
           k_norm, sb_bias, ret_norm, w_pool, pool_scale, w_out, norm_ffn, w_gate, w_up, w_down, router,
           w_gate_exp, w_up_exp, w_down_exp):
    bp, seq, d = x_prompt.shape
    db, dseq, _ = x_sample.shape
    depth = w_in.shape[0]
    n_pages = page_table.shape[1]
    past_len = n_pages * PAGE_SIZE
    h_sb = cache_k.shape[3]
    w_sb = h_sb * HEAD_DIM
    h_ret = state_ret.shape[2]
    w_ret = h_ret * HEAD_DIM
    w_pl = state_pool.shape[3]
    n_experts = router.shape[-1]
    np_, ns = bp * seq, db * dseq
    n = np_ + ns

    x = jnp.concatenate([x_prompt.reshape(np_, d), x_sample.reshape(ns, d)], axis=0)
    cache_kt = cache_k.transpose(0, 1, 3, 4, 2).reshape(depth, -1, w_sb, PAGE_SIZE)
    cache_vt = cache_v.transpose(0, 1, 3, 4, 2).reshape(depth, -1, w_sb, PAGE_SIZE)

    tm_big = _row_tile(n, 1280)
    tm_exp = 512

    cos_p, sin_p = _rope_tables(jnp.arange(seq))
    tabs_p = (cos_p, sin_p) + _decay_tables(RET_CHUNK, RET_CHUNK, h_ret)
    pos_s = past_len + jnp.arange(RET_CHUNK)
    cos_s, sin_s = _rope_tables(pos_s)
    tabs_s = (cos_s, sin_s) + _decay_tables(RET_CHUNK, dseq, h_ret)
    router_pad = jnp.pad(router, ((0, 0), (0, 0), (0, LANES - n_experts)))
    dense_tiles = jnp.zeros((n // tm_big,), jnp.int32)
    dense_valid = jnp.full((1,), n // tm_big, jnp.int32)

    outs = {k: [] for k in ("kp", "vp", "sp", "bp", "ks", "vs", "ss", "bs")}
    for l in range(depth):
        h = _rmsnorm(x, norm_mix, l)
        p = _matmul(h, w_in, l, tm=tm_big, tn=512, tk=d)

        qg = jnp.tile(q_norm[l], h_sb)[None, :]
        kg = jnp.tile(k_norm[l], h_sb)[None, :]
        qb, kf, kb, vb = _sb_prep(p, qg, kg, w_sb)
        tq = min(256, seq)
        mix = jnp.zeros((n, w_sb + w_ret + w_pl), BF16)
        mix = _sb_prompt(qb, kb, vb, sb_bias[l], bp, seq, mix, tq=tq, wide=min(2 * tq, seq))

        q_s = qb[np_:].reshape(db, dseq, h_sb // 2, 2, HEAD_DIM)
        eye2 = jnp.eye(2, dtype=BF16)
        q_pairs = jnp.einsum("btphd,hg->bphtgd", q_s, eye2).reshape(db, h_sb * dseq, LANES)
        bias_rows = jnp.broadcast_to(jnp.repeat(sb_bias[l], dseq)[:, None], (h_sb * dseq, LANES))
        k_s = kf[np_:].reshape(db, dseq, h_sb, HEAD_DIM)
        v_s = p[np_:, 2 * w_sb:3 * w_sb].reshape(db, dseq, h_sb, HEAD_DIM)
        pad = ((0, 0), (0, 0), (0, PAGE_SIZE - dseq))
        kt_new = jnp.pad(kf[np_:].reshape(db, dseq, w_sb).transpose(0, 2, 1), pad)
        vt_new = jnp.pad(p[np_:, 2 * w_sb:3 * w_sb].reshape(db, dseq, w_sb).transpose(0, 2, 1), pad)
        o_sb_s = _sb_sample(q_pairs, bias_rows, kt_new, vt_new, cache_kt, cache_vt, l,
                            page_table, pages_per_step=min(8, n_pages)).reshape(ns, w_sb)

        col_ret = 3 * w_sb
        gain_r = ret_norm[l].reshape(1, w_ret)
        s0_p = jnp.zeros((bp, h_ret // 2, LANES, LANES), F32)
        mix, s_p = _retention(p, 0, bp, seq, RET_CHUNK, col_ret, w_ret, tabs_p, s0_p, gain_r,
                              into=mix, into_col=w_sb // w_ret)
        p_s = jnp.pad(p[np_:, col_ret:col_ret + 4 * w_ret].reshape(db, dseq, 4 * w_ret),
                      ((0, 0), (0, RET_CHUNK - dseq), (0, 0))).reshape(db * RET_CHUNK, 4 * w_ret)
        o_r_s, s_s = _retention(p_s, 0, db, RET_CHUNK, RET_CHUNK, 0, w_ret, tabs_s,
                                _state_to_bd(state_ret[l]), gain_r)
        o_r_s = o_r_s.reshape(db, RET_CHUNK, w_ret)[:, :dseq].reshape(ns, w_ret)

        col_pool = col_ret + 4 * w_ret
        buf_p = jnp.zeros((bp, POOL_PAD, w_pl), F32)
        mix, nb_p = _pool(p, 0, bp, seq, col_pool, w_pl, buf_p, w_pool, pool_scale, l, 0, BF16,
                          into=mix, into_col=(w_sb + w_ret) // w_pl)
        buf_s = jnp.pad(state_pool[l], ((0, 0), (1, 0), (0, 0)))
        o_p_s, nb_s = _pool(p, np_, db, dseq, col_pool, w_pl, buf_s, w_pool, pool_scale, l, past_len, F32)

        mix_s = jnp.concatenate([o_sb_s.astype(BF16), o_r_s, o_p_s.astype(BF16)], axis=1)
        mix = lax.dynamic_update_slice(mix, mix_s, (np_, 0))
        x = _matmul(mix, w_out, l, res=x, tm=tm_big, tn=512, tk=d)

        i = l // 2
        if l % 2 == 0:
            h = _rmsnorm(x, norm_ffn, l)
            act = _gateup(h, w_gate[:, None], w_up[:, None], i, dense_tiles, dense_valid, tm=tm_big, tf=512)
            x = _matmul(act, w_down, i, res=x, tm=tm_big, tn=512, tk=act.shape[1] // 2)
        else:
            slab = _router(x, norm_ffn, l, router_pad[i:i + 1], n_experts)
            src_row, tile_expert, n_valid, pos1, pos2 = _routing(slab, n_experts, tm_exp)
            hs = _gather_norm(x, norm_ffn, l, src_row, n_valid * (tm_exp // 256), tg=256)
            act = _gateup(hs, w_gate_exp, w_up_exp, i, tile_expert, n_valid, tm=tm_exp, tf=512)
            dsorted = _down(act, w_down_exp, i, tile_expert, n_valid, tm=256, tn=512, sub=tm_exp // 256)
            x = _combine(x, dsorted, pos1, pos2, slab, tc=_row_tile(n, 128, 8))

        outs["kp"].append(kf[:np_].reshape(bp, seq, h_sb, HEAD_DIM))
        outs["vp"].append(p[:np_, 2 * w_sb:3 * w_sb].reshape(bp, seq, h_sb, HEAD_DIM))
        outs["sp"].append(_state_from_bd(s_p))
        outs["bp"].append(nb_p[:, 1:])
        outs["ks"].append(k_s)
        outs["vs"].append(v_s)
        outs["ss"].append(_state_from_bd(s_s))
        outs["bs"].append(nb_s[:, 1:])

    return (x[:np_].reshape(bp, seq, d), x[np_:].reshape(db, dseq, d),
            jnp.stack(outs["kp"]), jnp.stack(outs["vp"]), jnp.stack(outs["sp"]), jnp.stack(outs["bp"]),
            jnp.stack(outs["ks"]), jnp.stack(outs["vs"]), jnp.stack(outs["ss"]), jnp.stack(outs["bs"]))
```

```python
import functools

import jax
import jax.numpy as jnp
from jax import lax
from jax.experimental import pallas as pl
from jax.experimental.pallas import tpu as pltpu

F32 = jnp.float32
BF16 = jnp.bfloat16

HEAD_DIM = 64
LANES = 128
PAGE_SIZE = 128
POOL_WINDOWS = (2, 4, 8, 16)
POOL_BUF = max(POOL_WINDOWS) - 1
POOL_PAD = POOL_BUF + 1
RET_CHUNK = 128
ROPE_THETA = 10000.0
TOP_K = 2
EPS = 1e-6
LOG2E = 1.4426950408889634
VMEM_LIMIT = 52 * 1024 * 1024
DOWN_VMEM_LIMIT = 58 * 1024 * 1024


def _params(*sem):
    return pltpu.CompilerParams(dimension_semantics=sem, vmem_limit_bytes=VMEM_LIMIT)


def _row_tile(n, target, mult=16):
    best = None
    for t in range(mult, min(n, target) + 1, mult):
        if n % t == 0:
            best = t
    return best if best is not None else n


def _split_bf16(x):
    hi = x.astype(BF16)
    lo = (x - hi.astype(F32)).astype(BF16)
    return hi, lo


def _group_ones():
    r = lax.broadcasted_iota(jnp.int32, (LANES, LANES), 0) // HEAD_DIM
    c = lax.broadcasted_iota(jnp.int32, (LANES, LANES), 1) // HEAD_DIM
    return (r == c).astype(BF16)


def _head_sumsq(x, ones_bd):
    hi, lo = _split_bf16(x * x)
    return (jnp.dot(hi, ones_bd, preferred_element_type=F32)
            + jnp.dot(lo, ones_bd, preferred_element_type=F32))


def _dot_nt(a, b):
    return lax.dot_general(a, b, (((1,), (1,)), ((), ())), preferred_element_type=F32)


def _rms_kernel(x_ref, g_ref, o_ref):
    x = x_ref[...]
    ms = jnp.mean(x * x, axis=-1, keepdims=True)
    o_ref[...] = (x * lax.rsqrt(ms + EPS) * g_ref[0]).astype(o_ref.dtype)


def _rmsnorm(x, g_all, layer):
    n, d = x.shape
    tm = _row_tile(n, 640)
    return pl.pallas_call(
        _rms_kernel,
        grid=(n // tm,),
        in_specs=[pl.BlockSpec((tm, d), lambda i: (i, 0)),
                  pl.BlockSpec((1, 1, d), lambda i: (layer, 0, 0))],
        out_specs=pl.BlockSpec((tm, d), lambda i: (i, 0)),
        out_shape=jax.ShapeDtypeStruct((n, d), BF16),
        compiler_params=_params("parallel"),
        name="rmsnorm",
    )(x, g_all.reshape(g_all.shape[0], 1, d))


def _mm_kernel(a_ref, w_ref, *rest, nk, has_res):
    if has_res:
        r_ref, o_ref = rest
    else:
        (o_ref,) = rest
    prod = jnp.dot(a_ref[...], w_ref[...].astype(BF16), preferred_element_type=F32)
    if nk == 1:
        o_ref[...] = prod + r_ref[...] if has_res else prod
        return
    k = pl.program_id(2)

    @pl.when(k == 0)
    def _():
        o_ref[...] = prod + r_ref[...] if has_res else prod

    @pl.when(k > 0)
    def _():
        o_ref[...] += prod


def _matmul(a, w_all, layer, res=None, *, tm, tn, tk):
    m, kd = a.shape
    nd = w_all.shape[-1]
    nk = kd // tk
    in_specs = [pl.BlockSpec((tm, tk), lambda i, j, k: (i, k)),
                pl.BlockSpec((None, tk, tn), lambda i, j, k: (layer, k, j))]
    args = [a, w_all]
    if res is not None:
        in_specs.append(pl.BlockSpec((tm, tn), lambda i, j, k: (i, j)))
        args.append(res)
    return pl.pallas_call(
        functools.partial(_mm_kernel, nk=nk, has_res=res is not None),
        grid=(m // tm, nd // tn, nk),
        in_specs=in_specs,
        out_specs=pl.BlockSpec((tm, tn), lambda i, j, k: (i, j)),
        out_shape=jax.ShapeDtypeStruct((m, nd), F32),
        compiler_params=_params("parallel", "parallel", "arbitrary"),
        name="matmul",
    )(*args)


def _gateup_kernel(te_ref, nv_ref, a_ref, wg_ref, wu_ref, o_ref, wgb_ref, wub_ref):
    i = pl.program_id(1)

    @pl.when(jnp.logical_or(i == 0, te_ref[i] != te_ref[jnp.maximum(i - 1, 0)]))
    def _():
        wgb_ref[...] = wg_ref[...].astype(BF16)
        wub_ref[...] = wu_ref[...].astype(BF16)

    @pl.when(i < nv_ref[0])
    def _():
        a = a_ref[...]
        g = jnp.dot(a, wgb_ref[...], preferred_element_type=F32)
        u = jnp.dot(a, wub_ref[...], preferred_element_type=F32)
        o_ref[...] = (g * jax.nn.sigmoid(g) * u).astype(o_ref.dtype)

    @pl.when(i >= nv_ref[0])
    def _():
        o_ref[...] = jnp.zeros_like(o_ref)


def _gateup(a, wg_all, wu_all, layer, tile_expert, n_valid, *, tm, tf):
    r, kd = a.shape
    fd = wg_all.shape[-1]
    w_spec = pl.BlockSpec((None, None, kd, tf), lambda j, i, te, nv: (layer, te[i], 0, j))
    return pl.pallas_call(
        _gateup_kernel,
        grid_spec=pltpu.PrefetchScalarGridSpec(
            num_scalar_prefetch=2,
            grid=(fd // tf, r // tm),
            in_specs=[pl.BlockSpec((tm, kd), lambda j, i, te, nv: (jnp.minimum(i, nv[0] - 1), 0)),
                      w_spec, w_spec],
            out_specs=pl.BlockSpec((tm, tf), lambda j, i, te, nv: (i, j)),
            scratch_shapes=[pltpu.VMEM((kd, tf), BF16), pltpu.VMEM((kd, tf), BF16)],
        ),
        out_shape=jax.ShapeDtypeStruct((r, fd), BF16),
        compiler_params=pltpu.CompilerParams(dimension_semantics=("arbitrary", "arbitrary"),
                                             vmem_limit_bytes=DOWN_VMEM_LIMIT),
        name="gateup",
    )(tile_expert, n_valid, a, wg_all, wu_all)


def _down_kernel(te_ref, nv_ref, a_ref, w_ref, o_ref, wb_ref, *, sub):
    i = pl.program_id(1)
    valid = i < nv_ref[0] * sub

    @pl.when(jnp.logical_or(i == 0, te_ref[i // sub] != te_ref[jnp.maximum(i - 1, 0) // sub]))
    def _():
        wb_ref[...] = w_ref[...].astype(BF16)

    @pl.when(valid)
    def _():
        o_ref[...] = jnp.dot(a_ref[...], wb_ref[...], preferred_element_type=F32)

    @pl.when(jnp.logical_not(valid))
    def _():
        o_ref[...] = jnp.zeros_like(o_ref)


def _down(a, wd_all, layer, tile_expert, n_valid, *, tm, tn, sub):
    r, fd = a.shape
    dd = wd_all.shape[-1]

    def row(i, nv):
        return jnp.minimum(i, nv[0] * sub - 1)

    return pl.pallas_call(
        functools.partial(_down_kernel, sub=sub),
        grid_spec=pltpu.PrefetchScalarGridSpec(
            num_scalar_prefetch=2,
            grid=(dd // tn, r // tm),
            in_specs=[pl.BlockSpec((tm, fd), lambda j, i, te, nv: (row(i, nv), 0)),
                      pl.BlockSpec((None, None, fd, tn), lambda j, i, te, nv: (layer, te[i // sub], 0, j))],
            out_specs=pl.BlockSpec((tm, tn), lambda j, i, te, nv: (i, j)),
            scratch_shapes=[pltpu.VMEM((fd, tn), BF16)],
        ),
        out_shape=jax.ShapeDtypeStruct((r, dd), F32),
        compiler_params=pltpu.CompilerParams(dimension_semantics=("arbitrary", "arbitrary"),
                                             vmem_limit_bytes=DOWN_VMEM_LIMIT),
        name="expert_down",
    )(tile_expert, n_valid, a, wd_all)


def _router_kernel(x_ref, g_ref, r_ref, o_ref, *, n_experts):
    x = x_ref[...]
    ms = jnp.mean(x * x, axis=-1, keepdims=True)
    h = x * lax.rsqrt(ms + EPS) * g_ref[0]
    h_hi, h_lo = _split_bf16(h)
    r_hi, r_lo = _split_bf16(r_ref[0])
    logits = (jnp.dot(h_hi, r_hi, preferred_element_type=F32)
              + jnp.dot(h_hi, r_lo, preferred_element_type=F32)
              + jnp.dot(h_lo, r_hi, preferred_element_type=F32))
    lane = lax.broadcasted_iota(jnp.int32, logits.shape, 1)
    neg = jnp.float32(-jnp.inf)
    big = jnp.int32(LANES)
    logits = jnp.where(lane < n_experts, logits, neg)
    m1 = jnp.max(logits, axis=-1, keepdims=True)
    i1 = jnp.min(jnp.where(logits == m1, lane, big), axis=-1, keepdims=True)
    rest = jnp.where(lane == i1, neg, logits)
    m2 = jnp.max(rest, axis=-1, keepdims=True)
    i2 = jnp.min(jnp.where(rest == m2, lane, big), axis=-1, keepdims=True)
    e2 = jnp.exp(m2 - m1)
    den = 1.0 + e2
    g1 = 1.0 / den
    g2 = e2 / den
    out = jnp.where(lane == 0, i1.astype(F32),
                    jnp.where(lane == 1, i2.astype(F32),
                              jnp.where(lane == 2, g1, jnp.where(lane == 3, g2, 0.0))))
    o_ref[...] = out


def _router(x, g_all, layer, router_pad, n_experts):
    n, d = x.shape
    tm = _row_tile(n, 640)
    return pl.pallas_call(
        functools.partial(_router_kernel, n_experts=n_experts),
        grid=(n // tm,),
        in_specs=[pl.BlockSpec((tm, d), lambda i: (i, 0)),
                  pl.BlockSpec((1, 1, d), lambda i: (layer, 0, 0)),
                  pl.BlockSpec((1, d, LANES), lambda i: (0, 0, 0))],
        out_specs=pl.BlockSpec((tm, LANES), lambda i: (i, 0)),
        out_shape=jax.ShapeDtypeStruct((n, LANES), F32),
        compiler_params=_params("parallel"),
        name="router",
    )(x, g_all.reshape(g_all.shape[0], 1, d), router_pad)


def _start_row_gather(idx_ref, src_hbm, dst, sem, n_rows):
    def issue(r, c):
        pltpu.make_async_copy(src_hbm.at[pl.ds(idx_ref[0, 0, r], 1)], dst.at[pl.ds(r, 1)], sem).start()
        return c

    lax.fori_loop(0, n_rows, issue, 0)


def _wait_row_gather(src_hbm, dst, sem, n_rows):
    pltpu.make_async_copy(src_hbm.at[pl.ds(0, n_rows)], dst, sem).wait()


def _gather_norm_kernel(nv_ref, idx_ref, nxt_ref, x_hbm, g_ref, o_ref, buf, sem, *, tg):
    i = pl.program_id(0)
    nv = nv_ref[0]
    slot = i % 2

    @pl.when(jnp.logical_and(i == 0, nv > 0))
    def _():
        _start_row_gather(idx_ref, x_hbm, buf.at[0], sem.at[0], tg)

    @pl.when(i + 1 < nv)
    def _():
        _start_row_gather(nxt_ref, x_hbm, buf.at[1 - slot], sem.at[1 - slot], tg)

    @pl.when(i < nv)
    def _():
        _wait_row_gather(x_hbm, buf.at[slot], sem.at[slot], tg)
        x = buf[slot]
        ms = jnp.mean(x * x, axis=-1, keepdims=True)
        o_ref[...] = (x * lax.rsqrt(ms + EPS) * g_ref[0]).astype(o_ref.dtype)

    @pl.when(i >= nv)
    def _():
        o_ref[...] = jnp.zeros_like(o_ref)


def _gather_norm(x, g_all, layer, src_row, n_valid_tiles, *, tg):
    n, d = x.shape
    r = src_row.shape[0]
    nt = r // tg
    idx = src_row.reshape(nt, 1, tg)
    return pl.pallas_call(
        functools.partial(_gather_norm_kernel, tg=tg),
        grid_spec=pltpu.PrefetchScalarGridSpec(
            num_scalar_prefetch=1,
            grid=(nt,),
            in_specs=[pl.BlockSpec((1, 1, tg), lambda i, nv: (i, 0, 0), memory_space=pltpu.SMEM),
                      pl.BlockSpec((1, 1, tg), lambda i, nv: (jnp.minimum(i + 1, nt - 1), 0, 0),
                                   memory_space=pltpu.SMEM),
                      pl.BlockSpec(memory_space=pl.ANY),
                      pl.BlockSpec((1, 1, d), lambda i, nv: (layer, 0, 0))],
            out_specs=pl.BlockSpec((tg, d), lambda i, nv: (i, 0)),
            scratch_shapes=[pltpu.VMEM((2, tg, d), F32), pltpu.SemaphoreType.DMA((2,))],
        ),
        out_shape=jax.ShapeDtypeStruct((r, d), BF16),
        compiler_params=_params("arbitrary"),
        name="gather_norm",
    )(n_valid_tiles, idx, idx, x, g_all.reshape(g_all.shape[0], 1, d))


def _combine_kernel(idx_ref, nxt_ref, x_ref, g_ref, d_hbm, o_ref, buf, sem, *, tc):
    i = pl.program_id(0)
    slot = i % 2

    @pl.when(i == 0)
    def _():
        _start_row_gather(idx_ref, d_hbm, buf.at[0], sem.at[0], 2 * tc)

    @pl.when(i + 1 < pl.num_programs(0))
    def _():
        _start_row_gather(nxt_ref, d_hbm, buf.at[1 - slot], sem.at[1 - slot], 2 * tc)

    _wait_row_gather(d_hbm, buf.at[slot], sem.at[slot], 2 * tc)
    g = g_ref[...]
    g1 = g[:, TOP_K:TOP_K + 1]
    g2 = g[:, TOP_K + 1:TOP_K + 2]
    o_ref[...] = x_ref[...] + (g1 * buf[slot, 0:tc, :] + g2 * buf[slot, tc:2 * tc, :])


def _combine(x, dsorted, pos1, pos2, slab, *, tc):
    n, d = x.shape
    nt = n // tc
    idx = jnp.concatenate([pos1.reshape(nt, 1, tc), pos2.reshape(nt, 1, tc)], axis=2)
    return pl.pallas_call(
        functools.partial(_combine_kernel, tc=tc),
        grid=(nt,),
        in_specs=[pl.BlockSpec((1, 1, 2 * tc), lambda i: (i, 0, 0), memory_space=pltpu.SMEM),
                  pl.BlockSpec((1, 1, 2 * tc), lambda i: (jnp.minimum(i + 1, nt - 1), 0, 0),
                               memory_space=pltpu.SMEM),
                  pl.BlockSpec((tc, d), lambda i: (i, 0)),
                  pl.BlockSpec((tc, LANES), lambda i: (i, 0)),
                  pl.BlockSpec(memory_space=pl.ANY)],
        out_specs=pl.BlockSpec((tc, d), lambda i: (i, 0)),
        out_shape=jax.ShapeDtypeStruct((n, d), F32),
        scratch_shapes=[pltpu.VMEM((2, 2 * tc, d), F32), pltpu.SemaphoreType.DMA((2,))],
        compiler_params=_params("arbitrary"),
        name="combine",
    )(idx, idx, x, slab, dsorted)


def _sb_prep_kernel(q_ref, k_ref, v_ref, qg_ref, kg_ref, qb_ref, kf_ref, kb_ref, vb_ref, *, scale):
    ones_bd = _group_ones()
    w = q_ref.shape[1]
    for c in range(w // LANES):
        sl = slice(c * LANES, (c + 1) * LANES)
        q = q_ref[:, sl]
        qn = q * lax.rsqrt(_head_sumsq(q, ones_bd) * (1.0 / HEAD_DIM) + EPS) * qg_ref[:, sl]
        qb_ref[:, sl] = (qn * scale).astype(BF16)
        k = k_ref[:, sl]
        kn = k * lax.rsqrt(_head_sumsq(k, ones_bd) * (1.0 / HEAD_DIM) + EPS) * kg_ref[:, sl]
        kf_ref[:, sl] = kn
        kb_ref[:, sl] = kn.astype(BF16)
    vb_ref[...] = v_ref[...].astype(BF16)


def _sb_prep(p, qg, kg, w_sb):
    n = p.shape[0]
    tm = _row_tile(n, 640)
    blk = lambda c: pl.BlockSpec((tm, w_sb), lambda i: (i, c))
    gain = pl.BlockSpec((1, w_sb), lambda i: (0, 0))
    out = pl.BlockSpec((tm, w_sb), lambda i: (i, 0))
    return pl.pallas_call(
        functools.partial(_sb_prep_kernel, scale=HEAD_DIM ** -0.5),
        grid=(n // tm,),
        in_specs=[blk(0), blk(1), blk(2), gain, gain],
        out_specs=[out, out, out, out],
        out_shape=[jax.ShapeDtypeStruct((n, w_sb), BF16), jax.ShapeDtypeStruct((n, w_sb), F32),
                   jax.ShapeDtypeStruct((n, w_sb), BF16), jax.ShapeDtypeStruct((n, w_sb), BF16)],
        compiler_params=_params("parallel"),
        name="sb_prep",
    )(p, p, p, qg, kg)


def _softplus(z):
    return jnp.maximum(z, 0.0) + jnp.log(1.0 + jnp.exp2(jnp.abs(z) * (-LOG2E)))


def _suffix_matrix():
    r = lax.broadcasted_iota(jnp.int32, (2 * LANES, 2 * LANES), 0) % LANES
    c = lax.broadcasted_iota(jnp.int32, (2 * LANES, 2 * LANES), 1)
    return jnp.logical_or(r > c, c >= LANES).astype(BF16)


def _sb_block(z, carry, wmat, mask):
    sp, ts = _sb_suffix_sums(z, wmat, mask)
    return _sb_weights(z, sp, ts, carry, mask)


def _sb_suffix_sums(z, wmat, mask):
    sp = _softplus(z)
    if mask is not None:
        sp = jnp.where(mask, sp, 0.0)
    hi, lo = _split_bf16(sp)
    ts = []
    for s in range(z.shape[1] // LANES):
        sl = slice(s * LANES, (s + 1) * LANES)
        ts.append(jnp.dot(jnp.concatenate([hi[:, sl], lo[:, sl]], axis=1), wmat, preferred_element_type=F32))
    return sp, ts


def _sb_weights(z, sp, ts, carry, mask):
    n_sub = len(ts)
    tails = [None] * n_sub
    for s in range(n_sub - 1, -1, -1):
        tails[s] = ts[s][:, :LANES] + carry
        carry = carry + ts[s][:, LANES:]
    tail = tails[0] if n_sub == 1 else jnp.concatenate(tails, axis=1)
    a = jnp.exp(z - sp - tail)
    if mask is not None:
        a = jnp.where(mask, a, 0.0)
    return a, carry


def _sb_prompt_kernel(bias_ref, q_ref, k_ref, v_ref, o_ref, acc_ref, car_ref, *, tq, wide):
    pairs = q_ref.shape[1] // LANES
    heads = range(2 * pairs)
    hp0 = pl.program_id(1) * pairs
    qb = pl.program_id(2)
    lane = lax.broadcasted_iota(jnp.int32, (tq, LANES), 1)
    first = lane < HEAD_DIM
    psl = [slice(p * LANES, (p + 1) * LANES) for p in range(pairs)]
    qh, bh = [], []
    for p in range(pairs):
        q = q_ref[:, psl[p]]
        zero = jnp.zeros_like(q)
        qh += [jnp.where(first, q, zero), jnp.where(first, zero, q)]
        bh += [bias_ref[2 * (hp0 + p)], bias_ref[2 * (hp0 + p) + 1]]
    wmat = _suffix_matrix()
    acc_ref[...] = jnp.zeros_like(acc_ref)
    car_ref[...] = jnp.zeros_like(car_ref)

    def group(k0, width, mask):
        kblk = [k_ref[pl.ds(k0, width), psl[p]] for p in range(pairs)]
        vblk = [v_ref[pl.ds(k0, width), psl[p]] for p in range(pairs)]
        zs = [_dot_nt(qh[h], kblk[h // 2]) + bh[h] for h in heads]
        sums = [_sb_suffix_sums(z, wmat, mask) for z in zs]
        for h in heads:
            a, car = _sb_weights(zs[h], sums[h][0], sums[h][1], car_ref[h], mask)
            car_ref[h] = car
            acc_ref[h] += jnp.dot(a.astype(BF16), vblk[h // 2], preferred_element_type=F32)

    row = lax.broadcasted_iota(jnp.int32, (tq, tq), 0)
    col = lax.broadcasted_iota(jnp.int32, (tq, tq), 1)
    group(pl.multiple_of(qb * tq, tq), tq, col < row)
    n_wide = (qb * tq) // wide

    def full(j, c):
        group(pl.multiple_of(qb * tq - (j + 1) * wide, tq), wide, None)
        return c

    lax.fori_loop(0, n_wide, full, 0)
    if wide > tq:
        for r in range(wide // tq - 1, 0, -1):
            @pl.when(qb * tq - n_wide * wide >= r * tq)
            def _():
                group((r - 1) * tq, tq, None)

    for p in range(pairs):
        o_ref[:, psl[p]] = jnp.where(first, acc_ref[2 * p], acc_ref[2 * p + 1]).astype(o_ref.dtype)


def _sb_prompt(qb, kb, vb, bias, n_batch, seq, into, *, tq, wide, pairs):
    w = qb.shape[1]
    nq = seq // tq
    cw = pairs * LANES
    return pl.pallas_call(
        _drop_ref(functools.partial(_sb_prompt_kernel, tq=tq, wide=wide), 4),
        grid_spec=pltpu.PrefetchScalarGridSpec(
            num_scalar_prefetch=1,
            grid=(n_batch, w // cw, nq),
            in_specs=[pl.BlockSpec((tq, cw), lambda b, h, i, bias: (b * nq + i, h)),
                      pl.BlockSpec((seq, cw), lambda b, h, i, bias: (b, h)),
                      pl.BlockSpec((seq, cw), lambda b, h, i, bias: (b, h)),
                      pl.BlockSpec(memory_space=pl.ANY)],
            out_specs=pl.BlockSpec((tq, cw), lambda b, h, i, bias: (b * nq + i, h)),
            scratch_shapes=[pltpu.VMEM((2 * pairs, tq, LANES), F32), pltpu.VMEM((2 * pairs, tq, LANES), F32)],
        ),
        out_shape=jax.ShapeDtypeStruct(into.shape, into.dtype),
        input_output_aliases={4: 0},
        compiler_params=_params("parallel", "parallel", "arbitrary"),
        name="sb_prompt",
    )(bias, qb, kb, vb, into)


def _sb_sample_kernel(pt_ref, q_ref, bias_ref, kn_ref, vn_ref, *rest, n_heads, t_new, pages_per_step):
    page_refs = rest[:2 * pages_per_step]
    o_ref, acc_ref, car_ref = rest[2 * pages_per_step:]
    s = pl.program_id(1)
    n_pairs = n_heads // 2
    rows = n_heads * t_new
    wmat = _suffix_matrix()
    bias = bias_ref[...]

    rp = 2 * t_new

    def logits(kt_ref):
        zs = []
        for hp in range(n_pairs):
            kt = kt_ref[hp * LANES:(hp + 1) * LANES, :].astype(BF16)
            zs.append(jnp.dot(q_ref[0, hp * rp:(hp + 1) * rp, :], kt, preferred_element_type=F32))
        return jnp.concatenate(zs, axis=0) + bias

    def values(a, vt_ref):
        a = a.astype(BF16)
        outs = []
        for hp in range(n_pairs):
            vt = vt_ref[hp * LANES:(hp + 1) * LANES, :].astype(BF16)
            outs.append(_dot_nt(a[hp * rp:(hp + 1) * rp, :], vt))
        return jnp.concatenate(outs, axis=0)

    def pages(k_refs, v_refs, mask, acc, car):
        zs = [logits(r) for r in k_refs]
        sums = [_sb_suffix_sums(z, wmat, mask) for z in zs]
        weights = []
        for z, (ls, ts) in zip(zs, sums):
            a, car = _sb_weights(z, ls, ts, car, mask)
            weights.append(a)
        for a, r in zip(weights, v_refs):
            acc = acc + values(a, r)
        return acc, car

    @pl.when(s == 0)
    def _():
        t_idx = lax.broadcasted_iota(jnp.int32, (rows, PAGE_SIZE), 0) % t_new
        j_idx = lax.broadcasted_iota(jnp.int32, (rows, PAGE_SIZE), 1)
        zero = jnp.zeros((rows, LANES), F32)
        acc, car = pages([kn_ref.at[0]], [vn_ref.at[0]], j_idx < t_idx, zero, zero)
        acc_ref[...] = acc
        car_ref[...] = car

    acc, car = pages(page_refs[0::2], page_refs[1::2], None, acc_ref[...], car_ref[...])
    acc_ref[...] = acc
    car_ref[...] = car

    @pl.when(s == pl.num_programs(1) - 1)
    def _():
        lane = lax.broadcasted_iota(jnp.int32, (t_new, LANES), 1)
        for hp in range(n_pairs):
            base = hp * 2 * t_new
            o_ref[0, :, hp * LANES:(hp + 1) * LANES] = jnp.where(
                lane < HEAD_DIM, acc_ref[base:base + t_new, :],
                acc_ref[base + t_new:base + 2 * t_new, :]).astype(o_ref.dtype)


def _sb_sample(q_pairs, bias_rows, k_new, v_new, cache_k, cache_v, layer, page_table, *, pages_per_step):
    nb, n_pages = page_table.shape
    rows = q_pairs.shape[1]
    width = cache_k.shape[2]
    n_heads = width // HEAD_DIM
    t_new = rows // n_heads
    steps = n_pages // pages_per_step
    pt = page_table.reshape(-1)

    def page_spec(j):
        def imap(b, s, pt_ref):
            return (layer, pt_ref[b * n_pages + n_pages - 1 - (s * pages_per_step + j)], 0, 0)
        return pl.BlockSpec((None, None, width, PAGE_SIZE), imap)

    page_specs = []
    page_args = []
    for j in range(pages_per_step):
        page_specs += [page_spec(j), page_spec(j)]
        page_args += [cache_k, cache_v]
    new_spec = pl.BlockSpec((1, width, PAGE_SIZE), lambda b, s, pt_ref: (b, 0, 0))
    return pl.pallas_call(
        functools.partial(_sb_sample_kernel, n_heads=n_heads, t_new=t_new, pages_per_step=pages_per_step),
        grid_spec=pltpu.PrefetchScalarGridSpec(
            num_scalar_prefetch=1,
            grid=(nb, steps),
            in_specs=[pl.BlockSpec((1, rows, LANES), lambda b, s, pt_ref: (b, 0, 0)),
                      pl.BlockSpec((rows, LANES), lambda b, s, pt_ref: (0, 0)),
                      new_spec, new_spec] + page_specs,
            out_specs=pl.BlockSpec((1, t_new, n_heads * HEAD_DIM), lambda b, s, pt_ref: (b, 0, 0)),
            scratch_shapes=[pltpu.VMEM((rows, LANES), F32), pltpu.VMEM((rows, LANES), F32)],
        ),
        out_shape=jax.ShapeDtypeStruct((nb, t_new, n_heads * HEAD_DIM), F32),
        compiler_params=_params("parallel", "arbitrary"),
        name="sb_sample",
    )(pt, q_pairs, bias_rows, k_new, v_new, *page_args)


def _rope(x, cos, sin_signed, first_half):
    partner = jnp.where(first_half, pltpu.roll(x, LANES - HEAD_DIM // 2, 1), pltpu.roll(x, HEAD_DIM // 2, 1))
    return x * cos + partner * sin_signed


def _ret_kernel(q_ref, k_ref, v_ref, g_ref, cos_ref, sin_ref, dec_ref, qd_ref, kd_ref, cd_ref, s0_ref, rg_ref,
                o_ref, s_ref, st_ref):
    c = pl.program_id(1)
    rows = q_ref.shape[0]
    n_pairs = q_ref.shape[1] // LANES
    lane = lax.broadcasted_iota(jnp.int32, (rows, LANES), 1)
    first = lane < HEAD_DIM
    first_half = (lane % HEAD_DIM) < HEAD_DIM // 2
    r_i = lax.broadcasted_iota(jnp.int32, (LANES, LANES), 0) // HEAD_DIM
    c_i = lax.broadcasted_iota(jnp.int32, (LANES, LANES), 1) // HEAD_DIM
    same_head = r_i == c_i
    ones_bd = _group_ones()

    @pl.when(c == 0)
    def _():
        st_ref[...] = s0_ref[0]

    cos = cos_ref[...]
    sin = sin_ref[...]
    pairs = range(n_pairs)
    sl = [slice(h * LANES, (h + 1) * LANES) for h in pairs]
    k = [_rope(k_ref[:, sl[h]], cos, sin, first_half) * (HEAD_DIM ** -0.5) for h in pairs]
    qb = [_rope(q_ref[:, sl[h]], cos, sin, first_half).astype(BF16) for h in pairs]
    kb = [k[h].astype(BF16) for h in pairs]
    vb = [v_ref[:, sl[h]].astype(BF16) for h in pairs]
    zero = jnp.zeros_like(qb[0])
    state = [st_ref[h] for h in pairs]
    cross = [jnp.dot(qb[h], state[h].astype(BF16), preferred_element_type=F32) for h in pairs]
    att = [[_dot_nt(jnp.where(first, qb[h], zero) if hh == 0 else jnp.where(first, zero, qb[h]), kb[h])
            for hh in range(2)] for h in pairs]
    upd = [jnp.dot(jnp.transpose(k[h] * kd_ref[h]).astype(BF16), vb[h], preferred_element_type=F32)
           for h in pairs]
    inner = [[jnp.dot((att[h][hh] * dec_ref[h, hh]).astype(BF16), vb[h], preferred_element_type=F32)
              for hh in range(2)] for h in pairs]
    for h in pairs:
        new_state = state[h] * cd_ref[h] + jnp.where(same_head, upd[h], 0.0)
        st_ref[h] = new_state
        s_ref[0, h] = new_state
    o = [cross[h] * qd_ref[h] + jnp.where(first, inner[h][0], inner[h][1]) for h in pairs]
    ss = [_head_sumsq(o[h], ones_bd) for h in pairs]
    for h in pairs:
        y = o[h] * lax.rsqrt(ss[h] * (1.0 / HEAD_DIM) + EPS) * rg_ref[:, sl[h]]
        g = g_ref[:, sl[h]]
        o_ref[:, sl[h]] = (g * jax.nn.sigmoid(g) * y).astype(o_ref.dtype)


def _drop_ref(body, index):
    def wrapped(*refs):
        return body(*refs[:index], *refs[index + 1:])
    return wrapped


def _retention(p, row0, n_batch, seq, chunk, col0, w_ret, tabs, s0_bd, ret_gain, into=None, into_col=0):
    cos, sin, dec, qd, kd, cd = tabs
    n_pairs = w_ret // LANES
    nc = seq // chunk
    rb0 = row0 // chunk
    cb0 = col0 // w_ret

    def col(g):
        return pl.BlockSpec((chunk, w_ret), lambda b, c: (rb0 + b * nc + c, cb0 + g))

    def whole(a):
        return pl.BlockSpec(a.shape, lambda b, c: (0,) * a.ndim)

    in_specs = [col(0), col(1), col(2), col(3),
                pl.BlockSpec((chunk, LANES), lambda b, c: (c, 0)),
                pl.BlockSpec((chunk, LANES), lambda b, c: (c, 0)),
                whole(dec), whole(qd), whole(kd), whole(cd),
                pl.BlockSpec((1, n_pairs, LANES, LANES), lambda b, c: (b, 0, 0, 0)),
                whole(ret_gain)]
    args = [p, p, p, p, cos, sin, dec, qd, kd, cd, s0_bd, ret_gain]
    body, aliases = _ret_kernel, {}
    o_shape = jax.ShapeDtypeStruct((n_batch * seq, w_ret), BF16)
    if into is not None:
        body, aliases = _drop_ref(_ret_kernel, len(args)), {len(args): 0}
        in_specs.append(pl.BlockSpec(memory_space=pl.ANY))
        args.append(into)
        o_shape = jax.ShapeDtypeStruct(into.shape, into.dtype)
    return pl.pallas_call(
        body,
        grid=(n_batch, nc),
        in_specs=in_specs,
        out_specs=[pl.BlockSpec((chunk, w_ret), lambda b, c: (b * nc + c, into_col)),
                   pl.BlockSpec((1, n_pairs, LANES, LANES), lambda b, c: (b, 0, 0, 0))],
        out_shape=[o_shape, jax.ShapeDtypeStruct((n_batch, n_pairs, LANES, LANES), F32)],
        scratch_shapes=[pltpu.VMEM((n_pairs, LANES, LANES), F32)],
        input_output_aliases=aliases,
        compiler_params=_params("parallel", "arbitrary"),
        name="retention",
    )(*args)


def _rope_tables(pos):
    half = HEAD_DIM // 2
    inv = ROPE_THETA ** (-jnp.arange(half, dtype=F32) / half)
    ang = pos.astype(F32)[:, None] * inv[None, :]
    cos = jnp.cos(ang)
    sin = jnp.sin(ang)
    cos_t = jnp.tile(cos, (1, 2 * LANES // HEAD_DIM))
    sin_t = jnp.tile(jnp.concatenate([-sin, sin], axis=1), (1, LANES // HEAD_DIM))
    return cos_t, sin_t


def _decay_tables(rows, c_eff, n_heads):
    log_g = jnp.log(1.0 - 2.0 ** (-5.0 - jnp.arange(n_heads, dtype=F32)))
    idx = jnp.arange(rows, dtype=F32)
    diff = idx[:, None] - idx[None, :]
    decay = jnp.where(diff >= 0, jnp.exp(log_g[:, None, None] * jnp.maximum(diff, 0.0)), 0.0)
    q_dec = jnp.exp(log_g[:, None] * (idx + 1.0))
    k_dec = jnp.exp(log_g[:, None] * (c_eff - 1.0 - idx))
    chunk_dec = jnp.exp(log_g * c_eff)
    n_pairs = n_heads // 2

    def lanes(t):
        return jnp.repeat(t.reshape(n_pairs, 2, rows).transpose(0, 2, 1), HEAD_DIM, axis=2)

    dec = decay.reshape(n_pairs, 2, rows, rows)
    cd = jnp.repeat(chunk_dec.reshape(n_pairs, 2), HEAD_DIM, axis=1)
    cd = jnp.broadcast_to(cd[:, :, None], (n_pairs, LANES, LANES))
    return dec, lanes(q_dec), lanes(k_dec), cd


def _state_to_bd(s):
    b, h = s.shape[0], s.shape[1]
    s = s.reshape(b, h // 2, 2, HEAD_DIM, HEAD_DIM)
    z = jnp.zeros_like(s[:, :, 0])
    top = jnp.concatenate([s[:, :, 0], z], axis=-1)
    bot = jnp.concatenate([z, s[:, :, 1]], axis=-1)
    return jnp.concatenate([top, bot], axis=-2)


def _state_from_bd(s):
    b, hp = s.shape[0], s.shape[1]
    a = s[:, :, :HEAD_DIM, :HEAD_DIM]
    d = s[:, :, HEAD_DIM:, HEAD_DIM:]
    return jnp.stack([a, d], axis=2).reshape(b, hp * 2, HEAD_DIM, HEAD_DIM)


def _pool_kernel(u_ref, buf_ref, w_ref, sc_ref, o_ref, nb_ref, ext_ref, raw_ref, *, start, seq):
    u = u_ref[...]
    hist = buf_ref[0]
    raw_ref[0:POOL_PAD, :] = hist
    raw_ref[POOL_PAD:, :] = u
    r = lax.broadcasted_iota(jnp.int32, hist.shape, 0)
    ext_ref[0:POOL_PAD, :] = jnp.where(start - POOL_PAD + r >= 0, hist, 0.0)
    ext_ref[POOL_PAD:, :] = u
    nb_ref[0] = raw_ref[seq:seq + POOL_PAD, :]
    t = lax.broadcasted_iota(jnp.int32, (seq, LANES), 0)
    for g, w in enumerate(POOL_WINDOWS):
        sl = slice(g * LANES, (g + 1) * LANES)
        ssum = ext_ref[POOL_PAD:POOL_PAD + seq, sl]
        for j in range(1, w):
            ssum = ssum + ext_ref[POOL_PAD - j:POOL_PAD - j + seq, sl]
        n = jnp.minimum(start + t + 1, w).astype(F32)
        pooled = ssum / n - u[:, sl]
        y = jnp.dot(pooled.astype(BF16), w_ref[0, g].astype(BF16), preferred_element_type=F32)
        o_ref[:, sl] = (y * sc_ref[0, :, sl]).astype(o_ref.dtype)


def _pool(p, row0, n_batch, seq, col0, w_pool_w, buf16, w_pool_all, scale_all, layer, start, out_dtype,
          into=None, into_col=0):
    rb0 = row0 // seq
    cb0 = col0 // w_pool_w
    body = functools.partial(_pool_kernel, start=start, seq=seq)
    in_specs = [pl.BlockSpec((seq, w_pool_w), lambda b: (rb0 + b, cb0)),
                pl.BlockSpec((1, POOL_PAD, w_pool_w), lambda b: (b, 0, 0)),
                pl.BlockSpec((1, len(POOL_WINDOWS), LANES, LANES), lambda b: (layer, 0, 0, 0)),
                pl.BlockSpec((1, 1, w_pool_w), lambda b: (layer, 0, 0))]
    args = [p, buf16, w_pool_all, scale_all.reshape(scale_all.shape[0], 1, w_pool_w)]
    aliases = {}
    o_shape = jax.ShapeDtypeStruct((n_batch * seq, w_pool_w), out_dtype)
    if into is not None:
        body, aliases = _drop_ref(body, len(args)), {len(args): 0}
        in_specs.append(pl.BlockSpec(memory_space=pl.ANY))
        args.append(into)
        o_shape = jax.ShapeDtypeStruct(into.shape, into.dtype)
    return pl.pallas_call(
        body,
        grid=(n_batch,),
        in_specs=in_specs,
        out_specs=[pl.BlockSpec((seq, w_pool_w), lambda b: (b, into_col)),
                   pl.BlockSpec((1, POOL_PAD, w_pool_w), lambda b: (b, 0, 0))],
        out_shape=[o_shape, jax.ShapeDtypeStruct((n_batch, POOL_PAD, w_pool_w), F32)],
        scratch_shapes=[pltpu.VMEM((seq + POOL_PAD, w_pool_w), F32), pltpu.VMEM((seq + POOL_PAD, w_pool_w), F32)],
        input_output_aliases=aliases,
        compiler_params=_params("parallel"),
        name="pool",
    )(*args)


def _routing(slab, n_experts, tm):
    n = slab.shape[0]
    experts = slab[:, :TOP_K].astype(jnp.int32).reshape(-1)
    onehot = (experts[:, None] == jnp.arange(n_experts, dtype=jnp.int32)[None, :]).astype(jnp.int32)
    csum = jnp.cumsum(onehot, axis=0)
    rank = jnp.take_along_axis(csum - onehot, experts[:, None], axis=1)[:, 0]
    counts = csum[-1]
    padded = ((counts + tm - 1) // tm) * tm
    ends = jnp.cumsum(padded)
    starts = ends - padded
    pos = starts[experts] + rank
    n_rows = -(-(TOP_K * n + n_experts * (tm - 1)) // tm) * tm
    n_tiles = n_rows // tm
    token = jnp.arange(TOP_K * n, dtype=jnp.int32) // TOP_K
    src_row = jnp.zeros((n_rows,), jnp.int32).at[pos].set(token)
    tile_start = jnp.arange(n_tiles, dtype=jnp.int32) * tm
    tile_expert = jnp.sum((tile_start[:, None] >= ends[None, :]).astype(jnp.int32), axis=1)
    tile_expert = jnp.minimum(tile_expert, n_experts - 1)
    n_valid = (ends[-1] // tm).astype(jnp.int32).reshape(1)
    pos = pos.reshape(n, TOP_K)
    return src_row, tile_expert, n_valid, pos[:, 0], pos[:, 1]


def kernel(x_prompt, x_sample, cache_k, cache_v, state_ret, state_pool, page_table, norm_mix, w_in, q_norm,
           k_norm, sb_bias, ret_norm, w_pool, pool_scale, w_out, norm_ffn, w_gate, w_up, w_down, router,
           w_gate_exp, w_up_exp, w_down_exp):
    bp, seq, d = x_prompt.shape
    db, dseq, _ = x_sample.shape
    depth = w_in.shape[0]
    n_pages = page_table.shape[1]
    past_len = n_pages * PAGE_SIZE
    h_sb = cache_k.shape[3]
    w_sb = h_sb * HEAD_DIM
    h_ret = state_ret.shape[2]
    w_ret = h_ret * HEAD_DIM
    w_pl = state_pool.shape[3]
    n_experts = router.shape[-1]
    np_, ns = bp * seq, db * dseq
    n = np_ + ns

    x = jnp.concatenate([x_prompt.reshape(np_, d), x_sample.reshape(ns, d)], axis=0)
    cache_kt = cache_k.transpose(0, 1, 3, 4, 2).reshape(depth, -1, w_sb, PAGE_SIZE)
    cache_vt = cache_v.transpose(0, 1, 3, 4, 2).reshape(depth, -1, w_sb, PAGE_SIZE)

    tm_big = _row_tile(n, 1280)
    tm_exp = 512

    cos_p, sin_p = _rope_tables(jnp.arange(seq))
    tabs_p = (cos_p, sin_p) + _decay_tables(RET_CHUNK, RET_CHUNK, h_ret)
    pos_s = past_len + jnp.arange(RET_CHUNK)
    cos_s, sin_s = _rope_tables(pos_s)
    tabs_s = (cos_s, sin_s) + _decay_tables(RET_CHUNK, dseq, h_ret)
    router_pad = jnp.pad(router, ((0, 0), (0, 0), (0, LANES - n_experts)))
    dense_tiles = jnp.zeros((n // tm_big,), jnp.int32)
    dense_valid = jnp.full((1,), n // tm_big, jnp.int32)

    outs = {k: [] for k in ("kp", "vp", "sp", "bp", "ks", "vs", "ss", "bs")}
    for l in range(depth):
        h = _rmsnorm(x, norm_mix, l)
        p = _matmul(h, w_in, l, tm=tm_big, tn=512, tk=d)

        qg = jnp.tile(q_norm[l], h_sb)[None, :]
        kg = jnp.tile(k_norm[l], h_sb)[None, :]
        qb, kf, kb, vb = _sb_prep(p, qg, kg, w_sb)
        tq = min(256, seq)
        mix = jnp.zeros((n, w_sb + w_ret + w_pl), BF16)
        mix = _sb_prompt(qb, kb, vb, sb_bias[l], bp, seq, mix, tq=tq, wide=min(2 * tq, seq), pairs=4)

        q_s = qb[np_:].reshape(db, dseq, h_sb // 2, 2, HEAD_DIM)
        eye2 = jnp.eye(2, dtype=BF16)
        q_pairs = jnp.einsum("btphd,hg->bphtgd", q_s, eye2).reshape(db, h_sb * dseq, LANES)
        bias_rows = jnp.broadcast_to(jnp.repeat(sb_bias[l], dseq)[:, None], (h_sb * dseq, LANES))
        k_s = kf[np_:].reshape(db, dseq, h_sb, HEAD_DIM)
        v_s = p[np_:, 2 * w_sb:3 * w_sb].reshape(db, dseq, h_sb, HEAD_DIM)
        pad = ((0, 0), (0, 0), (0, PAGE_SIZE - dseq))
        kt_new = jnp.pad(kf[np_:].reshape(db, dseq, w_sb).transpose(0, 2, 1), pad)
        vt_new = jnp.pad(p[np_:, 2 * w_sb:3 * w_sb].reshape(db, dseq, w_sb).transpose(0, 2, 1), pad)
        o_sb_s = _sb_sample(q_pairs, bias_rows, kt_new, vt_new, cache_kt, cache_vt, l,
                            page_table, pages_per_step=min(8, n_pages)).reshape(ns, w_sb)

        col_ret = 3 * w_sb
        gain_r = ret_norm[l].reshape(1, w_ret)
        s0_p = jnp.zeros((bp, h_ret // 2, LANES, LANES), F32)
        mix, s_p = _retention(p, 0, bp, seq, RET_CHUNK, col_ret, w_ret, tabs_p, s0_p, gain_r,
                              into=mix, into_col=w_sb // w_ret)
        p_s = jnp.pad(p[np_:, col_ret:col_ret + 4 * w_ret].reshape(db, dseq, 4 * w_ret),
                      ((0, 0), (0, RET_CHUNK - dseq), (0, 0))).reshape(db * RET_CHUNK, 4 * w_ret)
        o_r_s, s_s = _retention(p_s, 0, db, RET_CHUNK, RET_CHUNK, 0, w_ret, tabs_s,
                                _state_to_bd(state_ret[l]), gain_r)
        o_r_s = o_r_s.reshape(db, RET_CHUNK, w_ret)[:, :dseq].reshape(ns, w_ret)

        col_pool = col_ret + 4 * w_ret
        buf_p = jnp.zeros((bp, POOL_PAD, w_pl), F32)
        mix, nb_p = _pool(p, 0, bp, seq, col_pool, w_pl, buf_p, w_pool, pool_scale, l, 0, BF16,
                          into=mix, into_col=(w_sb + w_ret) // w_pl)
        buf_s = jnp.pad(state_pool[l], ((0, 0), (1, 0), (0, 0)))
        o_p_s, nb_s = _pool(p, np_, db, dseq, col_pool, w_pl, buf_s, w_pool, pool_scale, l, past_len, F32)

        mix_s = jnp.concatenate([o_sb_s.astype(BF16), o_r_s, o_p_s.astype(BF16)], axis=1)
        mix = lax.dynamic_update_slice(mix, mix_s, (np_, 0))
        x = _matmul(mix, w_out, l, res=x, tm=tm_big, tn=512, tk=d)

        i = l // 2
        if l % 2 == 0:
            h = _rmsnorm(x, norm_ffn, l)
            act = _gateup(h, w_gate[:, None], w_up[:, None], i, dense_tiles, dense_valid, tm=tm_big, tf=512)
            x = _matmul(act, w_down, i, res=x, tm=tm_big, tn=512, tk=act.shape[1] // 2)
        else:
            slab = _router(x, norm_ffn, l, router_pad[i:i + 1], n_experts)
            src_row, tile_expert, n_valid, pos1, pos2 = _routing(slab, n_experts, tm_exp)
            hs = _gather_norm(x, norm_ffn, l, src_row, n_valid * (tm_exp // 256), tg=256)
            tf_exp = 1024 if w_gate_exp.shape[-1] % 1024 == 0 else 512
            act = _gateup(hs, w_gate_exp, w_up_exp, i, tile_expert, n_valid, tm=tm_exp, tf=tf_exp)
            dsorted = _down(act, w_down_exp, i, tile_expert, n_valid, tm=256, tn=512, sub=tm_exp // 256)
            x = _combine(x, dsorted, pos1, pos2, slab, tc=_row_tile(n, 128, 8))

        outs["kp"].append(kf[:np_].reshape(bp, seq, h_sb, HEAD_DIM))
        outs["vp"].append(p[:np_, 2 * w_sb:3 * w_sb].reshape(bp, seq, h_sb, HEAD_DIM))
        outs["sp"].append(_state_from_bd(s_p))
        outs["bp"].append(nb_p[:, 1:])
        outs["ks"].append(k_s)
        outs["vs"].append(v_s)
        outs["ss"].append(_state_from_bd(s_s))
        outs["bs"].append(nb_s[:, 1:])

    return (x[:np_].reshape(bp, seq, d), x[np_:].reshape(db, dseq, d),
            jnp.stack(outs["kp"]), jnp.stack(outs["vp"]), jnp.stack(outs["sp"]), jnp.stack(outs["bp"]),
            jnp.stack(outs["ks"]), jnp.stack(outs["vs"]), jnp.stack(outs["ss"]), jnp.stack(outs["bs"]))
```

```python
import functools

import jax
import jax.numpy as jnp
from jax import lax
from jax.experimental import pallas as pl
from jax.experimental.pallas import tpu as pltpu

F32 = jnp.float32
BF16 = jnp.bfloat16

HEAD_DIM = 64
LANES = 128
PAGE_SIZE = 128
POOL_WINDOWS = (2, 4, 8, 16)
POOL_BUF = max(POOL_WINDOWS) - 1
POOL_PAD = POOL_BUF + 1
RET_CHUNK = 128
ROPE_THETA = 10000.0
TOP_K = 2
EPS = 1e-6
LOG2E = 1.4426950408889634
VMEM_LIMIT = 52 * 1024 * 1024
DOWN_VMEM_LIMIT = 58 * 1024 * 1024


def _params(*sem):
    return pltpu.CompilerParams(dimension_semantics=sem, vmem_limit_bytes=VMEM_LIMIT)


def _row_tile(n, target, mult=16):
    best = None
    for t in range(mult, min(n, target) + 1, mult):
        if n % t == 0:
            best = t
    return best if best is not None else n


def _split_bf16(x):
    hi = x.astype(BF16)
    lo = (x - hi.astype(F32)).astype(BF16)
    return hi, lo


def _group_ones():
    r = lax.broadcasted_iota(jnp.int32, (LANES, LANES), 0) // HEAD_DIM
    c = lax.broadcasted_iota(jnp.int32, (LANES, LANES), 1) // HEAD_DIM
    return (r == c).astype(BF16)


def _head_sumsq(x, ones_bd):
    hi, lo = _split_bf16(x * x)
    return (jnp.dot(hi, ones_bd, preferred_element_type=F32)
            + jnp.dot(lo, ones_bd, preferred_element_type=F32))


def _dot_nt(a, b):
    return lax.dot_general(a, b, (((1,), (1,)), ((), ())), preferred_element_type=F32)


def _rms_kernel(x_ref, g_ref, o_ref):
    x = x_ref[...]
    ms = jnp.mean(x * x, axis=-1, keepdims=True)
    o_ref[...] = (x * lax.rsqrt(ms + EPS) * g_ref[0]).astype(o_ref.dtype)


def _rmsnorm(x, g_all, layer):
    n, d = x.shape
    tm = _row_tile(n, 640)
    return pl.pallas_call(
        _rms_kernel,
        grid=(n // tm,),
        in_specs=[pl.BlockSpec((tm, d), lambda i: (i, 0)),
                  pl.BlockSpec((1, 1, d), lambda i: (layer, 0, 0))],
        out_specs=pl.BlockSpec((tm, d), lambda i: (i, 0)),
        out_shape=jax.ShapeDtypeStruct((n, d), BF16),
        compiler_params=_params("parallel"),
        name="rmsnorm",
    )(x, g_all.reshape(g_all.shape[0], 1, d))


def _mm_kernel(a_ref, w_ref, *rest, nk, has_res):
    if has_res:
        r_ref, o_ref = rest
    else:
        (o_ref,) = rest
    prod = jnp.dot(a_ref[...], w_ref[...].astype(BF16), preferred_element_type=F32)
    if nk == 1:
        o_ref[...] = prod + r_ref[...] if has_res else prod
        return
    k = pl.program_id(2)

    @pl.when(k == 0)
    def _():
        o_ref[...] = prod + r_ref[...] if has_res else prod

    @pl.when(k > 0)
    def _():
        o_ref[...] += prod


def _matmul(a, w_all, layer, res=None, *, tm, tn, tk):
    m, kd = a.shape
    nd = w_all.shape[-1]
    nk = kd // tk
    in_specs = [pl.BlockSpec((tm, tk), lambda i, j, k: (i, k)),
                pl.BlockSpec((None, tk, tn), lambda i, j, k: (layer, k, j))]
    args = [a, w_all]
    if res is not None:
        in_specs.append(pl.BlockSpec((tm, tn), lambda i, j, k: (i, j)))
        args.append(res)
    return pl.pallas_call(
        functools.partial(_mm_kernel, nk=nk, has_res=res is not None),
        grid=(m // tm, nd // tn, nk),
        in_specs=in_specs,
        out_specs=pl.BlockSpec((tm, tn), lambda i, j, k: (i, j)),
        out_shape=jax.ShapeDtypeStruct((m, nd), F32),
        compiler_params=_params("parallel", "parallel", "arbitrary"),
        name="matmul",
    )(*args)


def _norm_mm_kernel(x_ref, g_ref, w_ref, o_ref, h_ref):
    @pl.when(pl.program_id(1) == 0)
    def _():
        x = x_ref[...]
        ms = jnp.mean(x * x, axis=-1, keepdims=True)
        h_ref[...] = (x * lax.rsqrt(ms + EPS) * g_ref[0]).astype(h_ref.dtype)

    o_ref[...] = jnp.dot(h_ref[...], w_ref[...].astype(BF16), preferred_element_type=F32)


def _norm_matmul(x, g_all, w_all, layer, *, tm, tn):
    m, kd = x.shape
    nd = w_all.shape[-1]
    return pl.pallas_call(
        _norm_mm_kernel,
        grid=(m // tm, nd // tn),
        in_specs=[pl.BlockSpec((tm, kd), lambda i, j: (i, 0)),
                  pl.BlockSpec((1, 1, kd), lambda i, j: (layer, 0, 0)),
                  pl.BlockSpec((None, kd, tn), lambda i, j: (layer, 0, j))],
        out_specs=pl.BlockSpec((tm, tn), lambda i, j: (i, j)),
        out_shape=jax.ShapeDtypeStruct((m, nd), F32),
        scratch_shapes=[pltpu.VMEM((tm, kd), BF16)],
        compiler_params=_params("parallel", "arbitrary"),
        name="norm_matmul",
    )(x, g_all.reshape(g_all.shape[0], 1, kd), w_all)


def _gateup_kernel(te_ref, nv_ref, a_ref, wg_ref, wu_ref, o_ref, wgb_ref, wub_ref):
    i = pl.program_id(1)

    @pl.when(jnp.logical_or(i == 0, te_ref[i] != te_ref[jnp.maximum(i - 1, 0)]))
    def _():
        wgb_ref[...] = wg_ref[...].astype(BF16)
        wub_ref[...] = wu_ref[...].astype(BF16)

    @pl.when(i < nv_ref[0])
    def _():
        a = a_ref[...]
        g = jnp.dot(a, wgb_ref[...], preferred_element_type=F32)
        u = jnp.dot(a, wub_ref[...], preferred_element_type=F32)
        o_ref[...] = (g * jax.nn.sigmoid(g) * u).astype(o_ref.dtype)

    @pl.when(i >= nv_ref[0])
    def _():
        o_ref[...] = jnp.zeros_like(o_ref)


def _gateup(a, wg_all, wu_all, layer, tile_expert, n_valid, *, tm, tf):
    r, kd = a.shape
    fd = wg_all.shape[-1]
    w_spec = pl.BlockSpec((None, None, kd, tf), lambda j, i, te, nv: (layer, te[i], 0, j))
    return pl.pallas_call(
        _gateup_kernel,
        grid_spec=pltpu.PrefetchScalarGridSpec(
            num_scalar_prefetch=2,
            grid=(fd // tf, r // tm),
            in_specs=[pl.BlockSpec((tm, kd), lambda j, i, te, nv: (jnp.minimum(i, nv[0] - 1), 0)),
                      w_spec, w_spec],
            out_specs=pl.BlockSpec((tm, tf), lambda j, i, te, nv: (i, j)),
            scratch_shapes=[pltpu.VMEM((kd, tf), BF16), pltpu.VMEM((kd, tf), BF16)],
        ),
        out_shape=jax.ShapeDtypeStruct((r, fd), BF16),
        compiler_params=pltpu.CompilerParams(dimension_semantics=("arbitrary", "arbitrary"),
                                             vmem_limit_bytes=DOWN_VMEM_LIMIT),
        name="gateup",
    )(tile_expert, n_valid, a, wg_all, wu_all)


def _down_kernel(te_ref, nv_ref, a_ref, w_ref, o_ref, wb_ref, *, sub):
    i = pl.program_id(1)
    valid = i < nv_ref[0] * sub

    @pl.when(jnp.logical_or(i == 0, te_ref[i // sub] != te_ref[jnp.maximum(i - 1, 0) // sub]))
    def _():
        wb_ref[...] = w_ref[...].astype(BF16)

    @pl.when(valid)
    def _():
        o_ref[...] = jnp.dot(a_ref[...], wb_ref[...], preferred_element_type=F32)

    @pl.when(jnp.logical_not(valid))
    def _():
        o_ref[...] = jnp.zeros_like(o_ref)


def _down(a, wd_all, layer, tile_expert, n_valid, *, tm, tn, sub):
    r, fd = a.shape
    dd = wd_all.shape[-1]

    def row(i, nv):
        return jnp.minimum(i, nv[0] * sub - 1)

    return pl.pallas_call(
        functools.partial(_down_kernel, sub=sub),
        grid_spec=pltpu.PrefetchScalarGridSpec(
            num_scalar_prefetch=2,
            grid=(dd // tn, r // tm),
            in_specs=[pl.BlockSpec((tm, fd), lambda j, i, te, nv: (row(i, nv), 0)),
                      pl.BlockSpec((None, None, fd, tn), lambda j, i, te, nv: (layer, te[i // sub], 0, j))],
            out_specs=pl.BlockSpec((tm, tn), lambda j, i, te, nv: (i, j)),
            scratch_shapes=[pltpu.VMEM((fd, tn), BF16)],
        ),
        out_shape=jax.ShapeDtypeStruct((r, dd), F32),
        compiler_params=pltpu.CompilerParams(dimension_semantics=("arbitrary", "arbitrary"),
                                             vmem_limit_bytes=DOWN_VMEM_LIMIT),
        name="expert_down",
    )(tile_expert, n_valid, a, wd_all)


def _router_kernel(x_ref, g_ref, r_ref, o_ref, *, n_experts):
    x = x_ref[...]
    ms = jnp.mean(x * x, axis=-1, keepdims=True)
    h = x * lax.rsqrt(ms + EPS) * g_ref[0]
    h_hi, h_lo = _split_bf16(h)
    r_hi, r_lo = _split_bf16(r_ref[0])
    logits = (jnp.dot(h_hi, r_hi, preferred_element_type=F32)
              + jnp.dot(h_hi, r_lo, preferred_element_type=F32)
              + jnp.dot(h_lo, r_hi, preferred_element_type=F32))
    lane = lax.broadcasted_iota(jnp.int32, logits.shape, 1)
    neg = jnp.float32(-jnp.inf)
    big = jnp.int32(LANES)
    logits = jnp.where(lane < n_experts, logits, neg)
    m1 = jnp.max(logits, axis=-1, keepdims=True)
    i1 = jnp.min(jnp.where(logits == m1, lane, big), axis=-1, keepdims=True)
    rest = jnp.where(lane == i1, neg, logits)
    m2 = jnp.max(rest, axis=-1, keepdims=True)
    i2 = jnp.min(jnp.where(rest == m2, lane, big), axis=-1, keepdims=True)
    e2 = jnp.exp(m2 - m1)
    den = 1.0 + e2
    g1 = 1.0 / den
    g2 = e2 / den
    out = jnp.where(lane == 0, i1.astype(F32),
                    jnp.where(lane == 1, i2.astype(F32),
                              jnp.where(lane == 2, g1, jnp.where(lane == 3, g2, 0.0))))
    o_ref[...] = out


def _router(x, g_all, layer, router_pad, n_experts):
    n, d = x.shape
    tm = _row_tile(n, 640)
    return pl.pallas_call(
        functools.partial(_router_kernel, n_experts=n_experts),
        grid=(n // tm,),
        in_specs=[pl.BlockSpec((tm, d), lambda i: (i, 0)),
                  pl.BlockSpec((1, 1, d), lambda i: (layer, 0, 0)),
                  pl.BlockSpec((1, d, LANES), lambda i: (0, 0, 0))],
        out_specs=pl.BlockSpec((tm, LANES), lambda i: (i, 0)),
        out_shape=jax.ShapeDtypeStruct((n, LANES), F32),
        compiler_params=_params("parallel"),
        name="router",
    )(x, g_all.reshape(g_all.shape[0], 1, d), router_pad)


def _start_row_gather(idx_ref, src_hbm, dst, sem, n_rows):
    def issue(r, c):
        pltpu.make_async_copy(src_hbm.at[pl.ds(idx_ref[0, 0, r], 1)], dst.at[pl.ds(r, 1)], sem).start()
        return c

    lax.fori_loop(0, n_rows, issue, 0)


def _wait_row_gather(src_hbm, dst, sem, n_rows):
    pltpu.make_async_copy(src_hbm.at[pl.ds(0, n_rows)], dst, sem).wait()


def _gather_norm_kernel(nv_ref, idx_ref, nxt_ref, x_hbm, g_ref, o_ref, buf, sem, *, tg):
    i = pl.program_id(0)
    nv = nv_ref[0]
    slot = i % 2

    @pl.when(jnp.logical_and(i == 0, nv > 0))
    def _():
        _start_row_gather(idx_ref, x_hbm, buf.at[0], sem.at[0], tg)

    @pl.when(i + 1 < nv)
    def _():
        _start_row_gather(nxt_ref, x_hbm, buf.at[1 - slot], sem.at[1 - slot], tg)

    @pl.when(i < nv)
    def _():
        _wait_row_gather(x_hbm, buf.at[slot], sem.at[slot], tg)
        x = buf[slot]
        ms = jnp.mean(x * x, axis=-1, keepdims=True)
        o_ref[...] = (x * lax.rsqrt(ms + EPS) * g_ref[0]).astype(o_ref.dtype)

    @pl.when(i >= nv)
    def _():
        o_ref[...] = jnp.zeros_like(o_ref)


def _gather_norm(x, g_all, layer, src_row, n_valid_tiles, *, tg):
    n, d = x.shape
    r = src_row.shape[0]
    nt = r // tg
    idx = src_row.reshape(nt, 1, tg)
    return pl.pallas_call(
        functools.partial(_gather_norm_kernel, tg=tg),
        grid_spec=pltpu.PrefetchScalarGridSpec(
            num_scalar_prefetch=1,
            grid=(nt,),
            in_specs=[pl.BlockSpec((1, 1, tg), lambda i, nv: (i, 0, 0), memory_space=pltpu.SMEM),
                      pl.BlockSpec((1, 1, tg), lambda i, nv: (jnp.minimum(i + 1, nt - 1), 0, 0),
                                   memory_space=pltpu.SMEM),
                      pl.BlockSpec(memory_space=pl.ANY),
                      pl.BlockSpec((1, 1, d), lambda i, nv: (layer, 0, 0))],
            out_specs=pl.BlockSpec((tg, d), lambda i, nv: (i, 0)),
            scratch_shapes=[pltpu.VMEM((2, tg, d), F32), pltpu.SemaphoreType.DMA((2,))],
        ),
        out_shape=jax.ShapeDtypeStruct((r, d), BF16),
        compiler_params=_params("arbitrary"),
        name="gather_norm",
    )(n_valid_tiles, idx, idx, x, g_all.reshape(g_all.shape[0], 1, d))


def _combine_kernel(idx_ref, nxt_ref, x_ref, g_ref, d_hbm, o_ref, buf, sem, *, tc):
    i = pl.program_id(0)
    slot = i % 2

    @pl.when(i == 0)
    def _():
        _start_row_gather(idx_ref, d_hbm, buf.at[0], sem.at[0], 2 * tc)

    @pl.when(i + 1 < pl.num_programs(0))
    def _():
        _start_row_gather(nxt_ref, d_hbm, buf.at[1 - slot], sem.at[1 - slot], 2 * tc)

    _wait_row_gather(d_hbm, buf.at[slot], sem.at[slot], 2 * tc)
    g = g_ref[...]
    g1 = g[:, TOP_K:TOP_K + 1]
    g2 = g[:, TOP_K + 1:TOP_K + 2]
    o_ref[...] = x_ref[...] + (g1 * buf[slot, 0:tc, :] + g2 * buf[slot, tc:2 * tc, :])


def _combine(x, dsorted, pos1, pos2, slab, *, tc):
    n, d = x.shape
    nt = n // tc
    idx = jnp.concatenate([pos1.reshape(nt, 1, tc), pos2.reshape(nt, 1, tc)], axis=2)
    return pl.pallas_call(
        functools.partial(_combine_kernel, tc=tc),
        grid=(nt,),
        in_specs=[pl.BlockSpec((1, 1, 2 * tc), lambda i: (i, 0, 0), memory_space=pltpu.SMEM),
                  pl.BlockSpec((1, 1, 2 * tc), lambda i: (jnp.minimum(i + 1, nt - 1), 0, 0),
                               memory_space=pltpu.SMEM),
                  pl.BlockSpec((tc, d), lambda i: (i, 0)),
                  pl.BlockSpec((tc, LANES), lambda i: (i, 0)),
                  pl.BlockSpec(memory_space=pl.ANY)],
        out_specs=pl.BlockSpec((tc, d), lambda i: (i, 0)),
        out_shape=jax.ShapeDtypeStruct((n, d), F32),
        scratch_shapes=[pltpu.VMEM((2, 2 * tc, d), F32), pltpu.SemaphoreType.DMA((2,))],
        compiler_params=_params("arbitrary"),
        name="combine",
    )(idx, idx, x, slab, dsorted)


def _sb_prep_kernel(q_ref, k_ref, v_ref, qg_ref, kg_ref, qb_ref, kf_ref, kb_ref, vb_ref, *, scale):
    ones_bd = _group_ones()
    w = q_ref.shape[1]
    for c in range(w // LANES):
        sl = slice(c * LANES, (c + 1) * LANES)
        q = q_ref[:, sl]
        qn = q * lax.rsqrt(_head_sumsq(q, ones_bd) * (1.0 / HEAD_DIM) + EPS) * qg_ref[:, sl]
        qb_ref[:, sl] = (qn * scale).astype(BF16)
        k = k_ref[:, sl]
        kn = k * lax.rsqrt(_head_sumsq(k, ones_bd) * (1.0 / HEAD_DIM) + EPS) * kg_ref[:, sl]
        kf_ref[:, sl] = kn
        kb_ref[:, sl] = kn.astype(BF16)
    vb_ref[...] = v_ref[...].astype(BF16)


def _sb_prep(p, qg, kg, w_sb):
    n = p.shape[0]
    tm = _row_tile(n, 640)
    blk = lambda c: pl.BlockSpec((tm, w_sb), lambda i: (i, c))
    gain = pl.BlockSpec((1, w_sb), lambda i: (0, 0))
    out = pl.BlockSpec((tm, w_sb), lambda i: (i, 0))
    return pl.pallas_call(
        functools.partial(_sb_prep_kernel, scale=HEAD_DIM ** -0.5),
        grid=(n // tm,),
        in_specs=[blk(0), blk(1), blk(2), gain, gain],
        out_specs=[out, out, out, out],
        out_shape=[jax.ShapeDtypeStruct((n, w_sb), BF16), jax.ShapeDtypeStruct((n, w_sb), F32),
                   jax.ShapeDtypeStruct((n, w_sb), BF16), jax.ShapeDtypeStruct((n, w_sb), BF16)],
        compiler_params=_params("parallel"),
        name="sb_prep",
    )(p, p, p, qg, kg)


def _softplus(z):
    return jnp.maximum(z, 0.0) + jnp.log(1.0 + jnp.exp2(jnp.abs(z) * (-LOG2E)))


def _suffix_matrix():
    r = lax.broadcasted_iota(jnp.int32, (2 * LANES, 2 * LANES), 0) % LANES
    c = lax.broadcasted_iota(jnp.int32, (2 * LANES, 2 * LANES), 1)
    return jnp.logical_or(r > c, c >= LANES).astype(BF16)


def _sb_block(z, carry, wmat, mask):
    sp, ts = _sb_suffix_sums(z, wmat, mask)
    return _sb_weights(z, sp, ts, carry, mask)


def _sb_suffix_sums(z, wmat, mask):
    sp = _softplus(z)
    if mask is not None:
        sp = jnp.where(mask, sp, 0.0)
    hi, lo = _split_bf16(sp)
    ts = []
    for s in range(z.shape[1] // LANES):
        sl = slice(s * LANES, (s + 1) * LANES)
        ts.append(jnp.dot(jnp.concatenate([hi[:, sl], lo[:, sl]], axis=1), wmat, preferred_element_type=F32))
    return sp, ts


def _sb_weights(z, sp, ts, carry, mask):
    n_sub = len(ts)
    tails = [None] * n_sub
    for s in range(n_sub - 1, -1, -1):
        tails[s] = ts[s][:, :LANES] + carry
        carry = carry + ts[s][:, LANES:]
    tail = tails[0] if n_sub == 1 else jnp.concatenate(tails, axis=1)
    a = jnp.exp(z - sp - tail)
    if mask is not None:
        a = jnp.where(mask, a, 0.0)
    return a, carry


def _sb_prompt_kernel(bias_ref, q_ref, k_ref, v_ref, o_ref, acc_ref, car_ref, *, tq, wide):
    pairs = q_ref.shape[1] // LANES
    heads = range(2 * pairs)
    hp0 = pl.program_id(1) * pairs
    qb = pl.program_id(2)
    lane = lax.broadcasted_iota(jnp.int32, (tq, LANES), 1)
    first = lane < HEAD_DIM
    psl = [slice(p * LANES, (p + 1) * LANES) for p in range(pairs)]
    qh, bh = [], []
    for p in range(pairs):
        q = q_ref[:, psl[p]]
        zero = jnp.zeros_like(q)
        qh += [jnp.where(first, q, zero), jnp.where(first, zero, q)]
        bh += [bias_ref[2 * (hp0 + p)], bias_ref[2 * (hp0 + p) + 1]]
    wmat = _suffix_matrix()
    acc_ref[...] = jnp.zeros_like(acc_ref)
    car_ref[...] = jnp.zeros_like(car_ref)

    def group(k0, width, mask):
        kblk = [k_ref[pl.ds(k0, width), psl[p]] for p in range(pairs)]
        vblk = [v_ref[pl.ds(k0, width), psl[p]] for p in range(pairs)]
        zs = [_dot_nt(qh[h], kblk[h // 2]) + bh[h] for h in heads]
        sums = [_sb_suffix_sums(z, wmat, mask) for z in zs]
        for h in heads:
            a, car = _sb_weights(zs[h], sums[h][0], sums[h][1], car_ref[h], mask)
            car_ref[h] = car
            acc_ref[h] += jnp.dot(a.astype(BF16), vblk[h // 2], preferred_element_type=F32)

    row = lax.broadcasted_iota(jnp.int32, (tq, tq), 0)
    col = lax.broadcasted_iota(jnp.int32, (tq, tq), 1)
    group(pl.multiple_of(qb * tq, tq), tq, col < row)
    n_wide = (qb * tq) // wide

    def full(j, c):
        group(pl.multiple_of(qb * tq - (j + 1) * wide, tq), wide, None)
        return c

    lax.fori_loop(0, n_wide, full, 0)
    if wide > tq:
        for r in range(wide // tq - 1, 0, -1):
            @pl.when(qb * tq - n_wide * wide >= r * tq)
            def _():
                group((r - 1) * tq, tq, None)

    for p in range(pairs):
        o_ref[:, psl[p]] = jnp.where(first, acc_ref[2 * p], acc_ref[2 * p + 1]).astype(o_ref.dtype)


def _sb_prompt(qb, kb, vb, bias, n_batch, seq, into, *, tq, wide, pairs):
    w = qb.shape[1]
    nq = seq // tq
    cw = pairs * LANES
    return pl.pallas_call(
        _drop_ref(functools.partial(_sb_prompt_kernel, tq=tq, wide=wide), 4),
        grid_spec=pltpu.PrefetchScalarGridSpec(
            num_scalar_prefetch=1,
            grid=(n_batch, w // cw, nq),
            in_specs=[pl.BlockSpec((tq, cw), lambda b, h, i, bias: (b * nq + i, h)),
                      pl.BlockSpec((seq, cw), lambda b, h, i, bias: (b, h)),
                      pl.BlockSpec((seq, cw), lambda b, h, i, bias: (b, h)),
                      pl.BlockSpec(memory_space=pl.ANY)],
            out_specs=pl.BlockSpec((tq, cw), lambda b, h, i, bias: (b * nq + i, h)),
            scratch_shapes=[pltpu.VMEM((2 * pairs, tq, LANES), F32), pltpu.VMEM((2 * pairs, tq, LANES), F32)],
        ),
        out_shape=jax.ShapeDtypeStruct(into.shape, into.dtype),
        input_output_aliases={4: 0},
        compiler_params=_params("parallel", "parallel", "arbitrary"),
        name="sb_prompt",
    )(bias, qb, kb, vb, into)


def _sb_sample_kernel(pt_ref, q_ref, bias_ref, kn_ref, vn_ref, *rest, n_heads, t_new, pages_per_step):
    page_refs = rest[:2 * pages_per_step]
    o_ref, acc_ref, car_ref = rest[2 * pages_per_step:]
    s = pl.program_id(1)
    n_pairs = n_heads // 2
    rows = n_heads * t_new
    wmat = _suffix_matrix()
    bias = bias_ref[...]

    rp = 2 * t_new

    def logits(kt_ref):
        zs = []
        for hp in range(n_pairs):
            kt = kt_ref[hp * LANES:(hp + 1) * LANES, :].astype(BF16)
            zs.append(jnp.dot(q_ref[0, hp * rp:(hp + 1) * rp, :], kt, preferred_element_type=F32))
        return jnp.concatenate(zs, axis=0) + bias

    def values(a, vt_ref):
        a = a.astype(BF16)
        outs = []
        for hp in range(n_pairs):
            vt = vt_ref[hp * LANES:(hp + 1) * LANES, :].astype(BF16)
            outs.append(_dot_nt(a[hp * rp:(hp + 1) * rp, :], vt))
        return jnp.concatenate(outs, axis=0)

    def pages(k_refs, v_refs, mask, acc, car):
        zs = [logits(r) for r in k_refs]
        sums = [_sb_suffix_sums(z, wmat, mask) for z in zs]
        weights = []
        for z, (ls, ts) in zip(zs, sums):
            a, car = _sb_weights(z, ls, ts, car, mask)
            weights.append(a)
        for a, r in zip(weights, v_refs):
            acc = acc + values(a, r)
        return acc, car

    @pl.when(s == 0)
    def _():
        t_idx = lax.broadcasted_iota(jnp.int32, (rows, PAGE_SIZE), 0) % t_new
        j_idx = lax.broadcasted_iota(jnp.int32, (rows, PAGE_SIZE), 1)
        zero = jnp.zeros((rows, LANES), F32)
        acc, car = pages([kn_ref.at[0]], [vn_ref.at[0]], j_idx < t_idx, zero, zero)
        acc_ref[...] = acc
        car_ref[...] = car

    acc, car = pages(page_refs[0::2], page_refs[1::2], None, acc_ref[...], car_ref[...])
    acc_ref[...] = acc
    car_ref[...] = car

    @pl.when(s == pl.num_programs(1) - 1)
    def _():
        lane = lax.broadcasted_iota(jnp.int32, (t_new, LANES), 1)
        for hp in range(n_pairs):
            base = hp * 2 * t_new
            o_ref[0, :, hp * LANES:(hp + 1) * LANES] = jnp.where(
                lane < HEAD_DIM, acc_ref[base:base + t_new, :],
                acc_ref[base + t_new:base + 2 * t_new, :]).astype(o_ref.dtype)


def _sb_sample(q_pairs, bias_rows, k_new, v_new, cache_k, cache_v, layer, page_table, *, pages_per_step):
    nb, n_pages = page_table.shape
    rows = q_pairs.shape[1]
    width = cache_k.shape[2]
    n_heads = width // HEAD_DIM
    t_new = rows // n_heads
    steps = n_pages // pages_per_step
    pt = page_table.reshape(-1)

    def page_spec(j):
        def imap(b, s, pt_ref):
            return (layer, pt_ref[b * n_pages + n_pages - 1 - (s * pages_per_step + j)], 0, 0)
        return pl.BlockSpec((None, None, width, PAGE_SIZE), imap)

    page_specs = []
    page_args = []
    for j in range(pages_per_step):
        page_specs += [page_spec(j), page_spec(j)]
        page_args += [cache_k, cache_v]
    new_spec = pl.BlockSpec((1, width, PAGE_SIZE), lambda b, s, pt_ref: (b, 0, 0))
    return pl.pallas_call(
        functools.partial(_sb_sample_kernel, n_heads=n_heads, t_new=t_new, pages_per_step=pages_per_step),
        grid_spec=pltpu.PrefetchScalarGridSpec(
            num_scalar_prefetch=1,
            grid=(nb, steps),
            in_specs=[pl.BlockSpec((1, rows, LANES), lambda b, s, pt_ref: (b, 0, 0)),
                      pl.BlockSpec((rows, LANES), lambda b, s, pt_ref: (0, 0)),
                      new_spec, new_spec] + page_specs,
            out_specs=pl.BlockSpec((1, t_new, n_heads * HEAD_DIM), lambda b, s, pt_ref: (b, 0, 0)),
            scratch_shapes=[pltpu.VMEM((rows, LANES), F32), pltpu.VMEM((rows, LANES), F32)],
        ),
        out_shape=jax.ShapeDtypeStruct((nb, t_new, n_heads * HEAD_DIM), F32),
        compiler_params=_params("parallel", "arbitrary"),
        name="sb_sample",
    )(pt, q_pairs, bias_rows, k_new, v_new, *page_args)


def _rope(x, cos, sin_signed, first_half):
    partner = jnp.where(first_half, pltpu.roll(x, LANES - HEAD_DIM // 2, 1), pltpu.roll(x, HEAD_DIM // 2, 1))
    return x * cos + partner * sin_signed


def _ret_kernel(q_ref, k_ref, v_ref, g_ref, cos_ref, sin_ref, dec_ref, qd_ref, kd_ref, cd_ref, s0_ref, rg_ref,
                o_ref, s_ref, st_ref):
    c = pl.program_id(1)
    rows = q_ref.shape[0]
    n_pairs = q_ref.shape[1] // LANES
    lane = lax.broadcasted_iota(jnp.int32, (rows, LANES), 1)
    first = lane < HEAD_DIM
    first_half = (lane % HEAD_DIM) < HEAD_DIM // 2
    r_i = lax.broadcasted_iota(jnp.int32, (LANES, LANES), 0) // HEAD_DIM
    c_i = lax.broadcasted_iota(jnp.int32, (LANES, LANES), 1) // HEAD_DIM
    same_head = r_i == c_i
    ones_bd = _group_ones()

    @pl.when(c == 0)
    def _():
        st_ref[...] = s0_ref[0]

    cos = cos_ref[...]
    sin = sin_ref[...]
    pairs = range(n_pairs)
    sl = [slice(h * LANES, (h + 1) * LANES) for h in pairs]
    k = [_rope(k_ref[:, sl[h]], cos, sin, first_half) * (HEAD_DIM ** -0.5) for h in pairs]
    qb = [_rope(q_ref[:, sl[h]], cos, sin, first_half).astype(BF16) for h in pairs]
    kb = [k[h].astype(BF16) for h in pairs]
    vb = [v_ref[:, sl[h]].astype(BF16) for h in pairs]
    zero = jnp.zeros_like(qb[0])
    state = [st_ref[h] for h in pairs]
    cross = [jnp.dot(qb[h], state[h].astype(BF16), preferred_element_type=F32) for h in pairs]
    att = [[_dot_nt(jnp.where(first, qb[h], zero) if hh == 0 else jnp.where(first, zero, qb[h]), kb[h])
            for hh in range(2)] for h in pairs]
    upd = [jnp.dot(jnp.transpose(k[h] * kd_ref[h]).astype(BF16), vb[h], preferred_element_type=F32)
           for h in pairs]
    inner = [[jnp.dot((att[h][hh] * dec_ref[h, hh]).astype(BF16), vb[h], preferred_element_type=F32)
              for hh in range(2)] for h in pairs]
    for h in pairs:
        new_state = state[h] * cd_ref[h] + jnp.where(same_head, upd[h], 0.0)
        st_ref[h] = new_state
        s_ref[0, h] = new_state
    o = [cross[h] * qd_ref[h] + jnp.where(first, inner[h][0], inner[h][1]) for h in pairs]
    ss = [_head_sumsq(o[h], ones_bd) for h in pairs]
    for h in pairs:
        y = o[h] * lax.rsqrt(ss[h] * (1.0 / HEAD_DIM) + EPS) * rg_ref[:, sl[h]]
        g = g_ref[:, sl[h]]
        o_ref[:, sl[h]] = (g * jax.nn.sigmoid(g) * y).astype(o_ref.dtype)


def _drop_ref(body, index):
    def wrapped(*refs):
        return body(*refs[:index], *refs[index + 1:])
    return wrapped


def _retention(p, row0, n_batch, seq, chunk, col0, w_ret, tabs, s0_bd, ret_gain, into=None, into_col=0):
    cos, sin, dec, qd, kd, cd = tabs
    n_pairs = w_ret // LANES
    nc = seq // chunk
    rb0 = row0 // chunk
    cb0 = col0 // w_ret

    def col(g):
        return pl.BlockSpec((chunk, w_ret), lambda b, c: (rb0 + b * nc + c, cb0 + g))

    def whole(a):
        return pl.BlockSpec(a.shape, lambda b, c: (0,) * a.ndim)

    in_specs = [col(0), col(1), col(2), col(3),
                pl.BlockSpec((chunk, LANES), lambda b, c: (c, 0)),
                pl.BlockSpec((chunk, LANES), lambda b, c: (c, 0)),
                whole(dec), whole(qd), whole(kd), whole(cd),
                pl.BlockSpec((1, n_pairs, LANES, LANES), lambda b, c: (b, 0, 0, 0)),
                whole(ret_gain)]
    args = [p, p, p, p, cos, sin, dec, qd, kd, cd, s0_bd, ret_gain]
    body, aliases = _ret_kernel, {}
    o_shape = jax.ShapeDtypeStruct((n_batch * seq, w_ret), BF16)
    if into is not None:
        body, aliases = _drop_ref(_ret_kernel, len(args)), {len(args): 0}
        in_specs.append(pl.BlockSpec(memory_space=pl.ANY))
        args.append(into)
        o_shape = jax.ShapeDtypeStruct(into.shape, into.dtype)
    return pl.pallas_call(
        body,
        grid=(n_batch, nc),
        in_specs=in_specs,
        out_specs=[pl.BlockSpec((chunk, w_ret), lambda b, c: (b * nc + c, into_col)),
                   pl.BlockSpec((1, n_pairs, LANES, LANES), lambda b, c: (b, 0, 0, 0))],
        out_shape=[o_shape, jax.ShapeDtypeStruct((n_batch, n_pairs, LANES, LANES), F32)],
        scratch_shapes=[pltpu.VMEM((n_pairs, LANES, LANES), F32)],
        input_output_aliases=aliases,
        compiler_params=_params("parallel", "arbitrary"),
        name="retention",
    )(*args)


def _rope_tables(pos):
    half = HEAD_DIM // 2
    inv = ROPE_THETA ** (-jnp.arange(half, dtype=F32) / half)
    ang = pos.astype(F32)[:, None] * inv[None, :]
    cos = jnp.cos(ang)
    sin = jnp.sin(ang)
    cos_t = jnp.tile(cos, (1, 2 * LANES // HEAD_DIM))
    sin_t = jnp.tile(jnp.concatenate([-sin, sin], axis=1), (1, LANES // HEAD_DIM))
    return cos_t, sin_t


def _decay_tables(rows, c_eff, n_heads):
    log_g = jnp.log(1.0 - 2.0 ** (-5.0 - jnp.arange(n_heads, dtype=F32)))
    idx = jnp.arange(rows, dtype=F32)
    diff = idx[:, None] - idx[None, :]
    decay = jnp.where(diff >= 0, jnp.exp(log_g[:, None, None] * jnp.maximum(diff, 0.0)), 0.0)
    q_dec = jnp.exp(log_g[:, None] * (idx + 1.0))
    k_dec = jnp.exp(log_g[:, None] * (c_eff - 1.0 - idx))
    chunk_dec = jnp.exp(log_g * c_eff)
    n_pairs = n_heads // 2

    def lanes(t):
        return jnp.repeat(t.reshape(n_pairs, 2, rows).transpose(0, 2, 1), HEAD_DIM, axis=2)

    dec = decay.reshape(n_pairs, 2, rows, rows)
    cd = jnp.repeat(chunk_dec.reshape(n_pairs, 2), HEAD_DIM, axis=1)
    cd = jnp.broadcast_to(cd[:, :, None], (n_pairs, LANES, LANES))
    return dec, lanes(q_dec), lanes(k_dec), cd


def _state_to_bd(s):
    b, h = s.shape[0], s.shape[1]
    s = s.reshape(b, h // 2, 2, HEAD_DIM, HEAD_DIM)
    z = jnp.zeros_like(s[:, :, 0])
    top = jnp.concatenate([s[:, :, 0], z], axis=-1)
    bot = jnp.concatenate([z, s[:, :, 1]], axis=-1)
    return jnp.concatenate([top, bot], axis=-2)


def _state_from_bd(s):
    b, hp = s.shape[0], s.shape[1]
    a = s[:, :, :HEAD_DIM, :HEAD_DIM]
    d = s[:, :, HEAD_DIM:, HEAD_DIM:]
    return jnp.stack([a, d], axis=2).reshape(b, hp * 2, HEAD_DIM, HEAD_DIM)


def _pool_kernel(u_ref, buf_ref, w_ref, sc_ref, o_ref, nb_ref, ext_ref, raw_ref, *, start, seq):
    u = u_ref[...]
    hist = buf_ref[0]
    raw_ref[0:POOL_PAD, :] = hist
    raw_ref[POOL_PAD:, :] = u
    r = lax.broadcasted_iota(jnp.int32, hist.shape, 0)
    ext_ref[0:POOL_PAD, :] = jnp.where(start - POOL_PAD + r >= 0, hist, 0.0)
    ext_ref[POOL_PAD:, :] = u
    nb_ref[0] = raw_ref[seq:seq + POOL_PAD, :]
    t = lax.broadcasted_iota(jnp.int32, (seq, LANES), 0)
    for g, w in enumerate(POOL_WINDOWS):
        sl = slice(g * LANES, (g + 1) * LANES)
        ssum = ext_ref[POOL_PAD:POOL_PAD + seq, sl]
        for j in range(1, w):
            ssum = ssum + ext_ref[POOL_PAD - j:POOL_PAD - j + seq, sl]
        n = jnp.minimum(start + t + 1, w).astype(F32)
        pooled = ssum / n - u[:, sl]
        y = jnp.dot(pooled.astype(BF16), w_ref[0, g].astype(BF16), preferred_element_type=F32)
        o_ref[:, sl] = (y * sc_ref[0, :, sl]).astype(o_ref.dtype)


def _pool(p, row0, n_batch, seq, col0, w_pool_w, buf16, w_pool_all, scale_all, layer, start, out_dtype,
          into=None, into_col=0):
    rb0 = row0 // seq
    cb0 = col0 // w_pool_w
    body = functools.partial(_pool_kernel, start=start, seq=seq)
    in_specs = [pl.BlockSpec((seq, w_pool_w), lambda b: (rb0 + b, cb0)),
                pl.BlockSpec((1, POOL_PAD, w_pool_w), lambda b: (b, 0, 0)),
                pl.BlockSpec((1, len(POOL_WINDOWS), LANES, LANES), lambda b: (layer, 0, 0, 0)),
                pl.BlockSpec((1, 1, w_pool_w), lambda b: (layer, 0, 0))]
    args = [p, buf16, w_pool_all, scale_all.reshape(scale_all.shape[0], 1, w_pool_w)]
    aliases = {}
    o_shape = jax.ShapeDtypeStruct((n_batch * seq, w_pool_w), out_dtype)
    if into is not None:
        body, aliases = _drop_ref(body, len(args)), {len(args): 0}
        in_specs.append(pl.BlockSpec(memory_space=pl.ANY))
        args.append(into)
        o_shape = jax.ShapeDtypeStruct(into.shape, into.dtype)
    return pl.pallas_call(
        body,
        grid=(n_batch,),
        in_specs=in_specs,
        out_specs=[pl.BlockSpec((seq, w_pool_w), lambda b: (b, into_col)),
                   pl.BlockSpec((1, POOL_PAD, w_pool_w), lambda b: (b, 0, 0))],
        out_shape=[o_shape, jax.ShapeDtypeStruct((n_batch, POOL_PAD, w_pool_w), F32)],
        scratch_shapes=[pltpu.VMEM((seq + POOL_PAD, w_pool_w), F32), pltpu.VMEM((seq + POOL_PAD, w_pool_w), F32)],
        input_output_aliases=aliases,
        compiler_params=_params("parallel"),
        name="pool",
    )(*args)


def _routing(slab, n_experts, tm):
    n = slab.shape[0]
    experts = slab[:, :TOP_K].astype(jnp.int32).reshape(-1)
    onehot = (experts[:, None] == jnp.arange(n_experts, dtype=jnp.int32)[None, :]).astype(jnp.int32)
    csum = jnp.cumsum(onehot, axis=0)
    rank = jnp.take_along_axis(csum - onehot, experts[:, None], axis=1)[:, 0]
    counts = csum[-1]
    padded = ((counts + tm - 1) // tm) * tm
    ends = jnp.cumsum(padded)
    starts = ends - padded
    pos = starts[experts] + rank
    n_rows = -(-(TOP_K * n + n_experts * (tm - 1)) // tm) * tm
    n_tiles = n_rows // tm
    token = jnp.arange(TOP_K * n, dtype=jnp.int32) // TOP_K
    src_row = jnp.zeros((n_rows,), jnp.int32).at[pos].set(token)
    tile_start = jnp.arange(n_tiles, dtype=jnp.int32) * tm
    tile_expert = jnp.sum((tile_start[:, None] >= ends[None, :]).astype(jnp.int32), axis=1)
    tile_expert = jnp.minimum(tile_expert, n_experts - 1)
    n_valid = (ends[-1] // tm).astype(jnp.int32).reshape(1)
    pos = pos.reshape(n, TOP_K)
    return src_row, tile_expert, n_valid, pos[:, 0], pos[:, 1]


def kernel(x_prompt, x_sample, cache_k, cache_v, state_ret, state_pool, page_table, norm_mix, w_in, q_norm,
           k_norm, sb_bias, ret_norm, w_pool, pool_scale, w_out, norm_ffn, w_gate, w_up, w_down, router,
           w_gate_exp, w_up_exp, w_down_exp):
    bp, seq, d = x_prompt.shape
    db, dseq, _ = x_sample.shape
    depth = w_in.shape[0]
    n_pages = page_table.shape[1]
    past_len = n_pages * PAGE_SIZE
    h_sb = cache_k.shape[3]
    w_sb = h_sb * HEAD_DIM
    h_ret = state_ret.shape[2]
    w_ret = h_ret * HEAD_DIM
    w_pl = state_pool.shape[3]
    n_experts = router.shape[-1]
    np_, ns = bp * seq, db * dseq
    n = np_ + ns

    x = jnp.concatenate([x_prompt.reshape(np_, d), x_sample.reshape(ns, d)], axis=0)
    cache_kt = cache_k.transpose(0, 1, 3, 4, 2).reshape(depth, -1, w_sb, PAGE_SIZE)
    cache_vt = cache_v.transpose(0, 1, 3, 4, 2).reshape(depth, -1, w_sb, PAGE_SIZE)

    tm_big = _row_tile(n, 1280)
    tm_exp = 512

    cos_p, sin_p = _rope_tables(jnp.arange(seq))
    tabs_p = (cos_p, sin_p) + _decay_tables(RET_CHUNK, RET_CHUNK, h_ret)
    pos_s = past_len + jnp.arange(RET_CHUNK)
    cos_s, sin_s = _rope_tables(pos_s)
    tabs_s = (cos_s, sin_s) + _decay_tables(RET_CHUNK, dseq, h_ret)
    router_pad = jnp.pad(router, ((0, 0), (0, 0), (0, LANES - n_experts)))
    dense_tiles = jnp.zeros((n // tm_big,), jnp.int32)
    dense_valid = jnp.full((1,), n // tm_big, jnp.int32)

    outs = {k: [] for k in ("kp", "vp", "sp", "bp", "ks", "vs", "ss", "bs")}
    for l in range(depth):
        p = _norm_matmul(x, norm_mix, w_in, l, tm=tm_big, tn=512)

        qg = jnp.tile(q_norm[l], h_sb)[None, :]
        kg = jnp.tile(k_norm[l], h_sb)[None, :]
        qb, kf, kb, vb = _sb_prep(p, qg, kg, w_sb)
        tq = min(256, seq)
        mix = jnp.zeros((n, w_sb + w_ret + w_pl), BF16)
        mix = _sb_prompt(qb, kb, vb, sb_bias[l], bp, seq, mix, tq=tq, wide=min(2 * tq, seq), pairs=4)

        q_s = qb[np_:].reshape(db, dseq, h_sb // 2, 2, HEAD_DIM)
        eye2 = jnp.eye(2, dtype=BF16)
        q_pairs = jnp.einsum("btphd,hg->bphtgd", q_s, eye2).reshape(db, h_sb * dseq, LANES)
        bias_rows = jnp.broadcast_to(jnp.repeat(sb_bias[l], dseq)[:, None], (h_sb * dseq, LANES))
        k_s = kf[np_:].reshape(db, dseq, h_sb, HEAD_DIM)
        v_s = p[np_:, 2 * w_sb:3 * w_sb].reshape(db, dseq, h_sb, HEAD_DIM)
        pad = ((0, 0), (0, 0), (0, PAGE_SIZE - dseq))
        kt_new = jnp.pad(kf[np_:].reshape(db, dseq, w_sb).transpose(0, 2, 1), pad)
        vt_new = jnp.pad(p[np_:, 2 * w_sb:3 * w_sb].reshape(db, dseq, w_sb).transpose(0, 2, 1), pad)
        o_sb_s = _sb_sample(q_pairs, bias_rows, kt_new, vt_new, cache_kt, cache_vt, l,
                            page_table, pages_per_step=min(8, n_pages)).reshape(ns, w_sb)

        col_ret = 3 * w_sb
        gain_r = ret_norm[l].reshape(1, w_ret)
        s0_p = jnp.zeros((bp, h_ret // 2, LANES, LANES), F32)
        mix, s_p = _retention(p, 0, bp, seq, RET_CHUNK, col_ret, w_ret, tabs_p, s0_p, gain_r,
                              into=mix, into_col=w_sb // w_ret)
        p_s = jnp.pad(p[np_:, col_ret:col_ret + 4 * w_ret].reshape(db, dseq, 4 * w_ret),
                      ((0, 0), (0, RET_CHUNK - dseq), (0, 0))).reshape(db * RET_CHUNK, 4 * w_ret)
        o_r_s, s_s = _retention(p_s, 0, db, RET_CHUNK, RET_CHUNK, 0, w_ret, tabs_s,
                                _state_to_bd(state_ret[l]), gain_r)
        o_r_s = o_r_s.reshape(db, RET_CHUNK, w_ret)[:, :dseq].reshape(ns, w_ret)

        col_pool = col_ret + 4 * w_ret
        buf_p = jnp.zeros((bp, POOL_PAD, w_pl), F32)
        mix, nb_p = _pool(p, 0, bp, seq, col_pool, w_pl, buf_p, w_pool, pool_scale, l, 0, BF16,
                          into=mix, into_col=(w_sb + w_ret) // w_pl)
        buf_s = jnp.pad(state_pool[l], ((0, 0), (1, 0), (0, 0)))
        o_p_s, nb_s = _pool(p, np_, db, dseq, col_pool, w_pl, buf_s, w_pool, pool_scale, l, past_len, F32)

        mix_s = jnp.concatenate([o_sb_s.astype(BF16), o_r_s, o_p_s.astype(BF16)], axis=1)
        mix = lax.dynamic_update_slice(mix, mix_s, (np_, 0))
        x = _matmul(mix, w_out, l, res=x, tm=tm_big, tn=512, tk=d)

        i = l // 2
        if l % 2 == 0:
            h = _rmsnorm(x, norm_ffn, l)
            act = _gateup(h, w_gate[:, None], w_up[:, None], i, dense_tiles, dense_valid, tm=tm_big, tf=512)
            x = _matmul(act, w_down, i, res=x, tm=tm_big, tn=512, tk=act.shape[1] // 2)
        else:
            slab = _router(x, norm_ffn, l, router_pad[i:i + 1], n_experts)
            src_row, tile_expert, n_valid, pos1, pos2 = _routing(slab, n_experts, tm_exp)
            hs = _gather_norm(x, norm_ffn, l, src_row, n_valid * (tm_exp // 256), tg=256)
            tf_exp = 1024 if w_gate_exp.shape[-1] % 1024 == 0 else 512
            act = _gateup(hs, w_gate_exp, w_up_exp, i, tile_expert, n_valid, tm=tm_exp, tf=tf_exp)
            dsorted = _down(act, w_down_exp, i, tile_expert, n_valid, tm=tm_exp, tn=512, sub=1)
            x = _combine(x, dsorted, pos1, pos2, slab, tc=_row_tile(n, 128, 8))

        outs["kp"].append(kf[:np_].reshape(bp, seq, h_sb, HEAD_DIM))
        outs["vp"].append(p[:np_, 2 * w_sb:3 * w_sb].reshape(bp, seq, h_sb, HEAD_DIM))
        outs["sp"].append(_state_from_bd(s_p))
        outs["bp"].append(nb_p[:, 1:])
        outs["ks"].append(k_s)
        outs["vs"].append(v_s)
        outs["ss"].append(_state_from_bd(s_s))
        outs["bs"].append(nb_s[:, 1:])

    return (x[:np_].reshape(bp, seq, d), x[np_:].reshape(db, dseq, d),
            jnp.stack(outs["kp"]), jnp.stack(outs["vp"]), jnp.stack(outs["sp"]), jnp.stack(outs["bp"]),
            jnp.stack(outs["ks"]), jnp.stack(outs["vs"]), jnp.stack(outs["ss"]), jnp.stack(outs["bs"]))
```
